```python
import math
import jax, jax.numpy as jnp
from jax import lax
import numpy as np

D_MODEL = 2048
BATCH = 8
SEQ = 4096
DEPTH = 4

N_BRANCHES = 4
BRANCH_WIDTH = D_MODEL // N_BRANCHES
DSA_HEADS = 4
DSA_HEAD_DIM = BRANCH_WIDTH // DSA_HEADS
IDX_HEADS = 8
IDX_DIM = 64
INDEX_TOPK = 256
GLA_HEADS = 4
GLA_KEY_DIM = BRANCH_WIDTH // GLA_HEADS // 2
GLA_VAL_DIM = BRANCH_WIDTH // GLA_HEADS
GLA_GATE_RANK = 16
GLA_TAU = 16.0
GLA_CHUNK = 64
S5_GROUP = 16
S5_GROUPS = BRANCH_WIDTH // S5_GROUP
S5_STATE = 64
S5_DT_MIN = 1e-3
S5_DT_MAX = 1e-1
MLA_HEADS = 4
MLA_NOPE_DIM = 128
MLA_ROPE_DIM = 64
MLA_V_DIM = 128
MLA_Q_RANK = 384
MLA_KV_RANK = 128
D_FF = ((8 * D_MODEL // 3 + 127) // 128) * 128
ROPE_THETA = 10000.0
Q_BLOCK = 128
NORM_EPS = 1e-6
N_MOD = 9

IN_SIZES = (
    DSA_HEADS * DSA_HEAD_DIM, DSA_HEADS * DSA_HEAD_DIM, DSA_HEADS * DSA_HEAD_DIM,
    IDX_HEADS * IDX_DIM, IDX_DIM, IDX_HEADS,
    GLA_HEADS * GLA_KEY_DIM, GLA_HEADS * GLA_KEY_DIM, GLA_HEADS * GLA_VAL_DIM,
    GLA_GATE_RANK, GLA_HEADS * GLA_VAL_DIM,
    BRANCH_WIDTH,
    MLA_Q_RANK, MLA_KV_RANK, MLA_ROPE_DIM,
)
IN_COLS = int(sum(IN_SIZES))
IN_OFFSETS = tuple(int(o) for o in np.cumsum(IN_SIZES)[:-1])

kernel_name = 'hybrid_gated_dsa_gla_s5_mla_block'


def rms_norm(x, g):
    xf = x.astype(jnp.float32)
    y = xf * lax.rsqrt(jnp.mean(xf * xf, axis=-1, keepdims=True) + NORM_EPS)
    return (y * g.astype(jnp.float32)).astype(x.dtype)


def ada_norm(x, gain, shift, scale):
    return rms_norm(x, gain) * (1.0 + scale) + shift


def rope(x, positions):
    half = x.shape[-1] // 2
    inv_freq = jnp.power(ROPE_THETA, -jnp.arange(half, dtype=jnp.float32) / half)
    ang = positions.astype(jnp.float32)[:, :, None] * inv_freq
    cos = jnp.cos(ang)[:, :, None, :]
    sin = jnp.sin(ang)[:, :, None, :]
    xf = x.astype(jnp.float32)
    x1, x2 = xf[..., :half], xf[..., half:]
    return jnp.concatenate([x1 * cos - x2 * sin, x2 * cos + x1 * sin], axis=-1).astype(x.dtype)


def swiglu(h, w_in, w_out):
    a, b = jnp.split(h @ w_in, 2, axis=-1)
    return (jax.nn.silu(a) * b) @ w_out


def dsa_sparse_attention(q, k, v, iq, ik, iw, k_sel):
    bsz, seq, heads, hd = q.shape
    nb = seq // Q_BLOCK
    key_pos = jnp.arange(seq)

    def to_blocks(a):
        return a.reshape(bsz, nb, Q_BLOCK, *a.shape[2:]).swapaxes(0, 1)

    def block(args):
        qb, iqb, iwb, t = args
        s = jnp.einsum('bqhd,bsd->bqhs', iqb, ik).astype(jnp.float32) * IDX_DIM ** -0.5
        score = jnp.einsum('bqh,bqhs->bqs', iwb.astype(jnp.float32), jax.nn.relu(s))
        score = jnp.where(key_pos[None, None, :] <= t[None, :, None], score, -jnp.inf)
        _, idx = lax.top_k(score, k_sel)
        kg = jax.vmap(lambda kb, ib: kb[ib])(k, idx)
        vg = jax.vmap(lambda vb, ib: vb[ib])(v, idx)
        logits = jnp.einsum('bqhd,bqkhd->bhqk', qb, kg).astype(jnp.float32) * hd ** -0.5
        valid = (idx <= t[None, :, None])[:, None]
        logits = jnp.where(valid, logits, -jnp.inf)
        p = jax.nn.softmax(logits, axis=-1).astype(vg.dtype)
        return jnp.einsum('bhqk,bqkhd->bqhd', p, vg)

    out = lax.map(block, (to_blocks(q), to_blocks(iq), to_blocks(iw), key_pos.reshape(nb, Q_BLOCK)))
    return out.swapaxes(0, 1).reshape(bsz, seq, heads, hd)


def dsa_branch(a_q, a_k, a_v, i_q, i_k, i_w, positions, qk_norm, k_sel):
    bsz, seq, _ = a_q.shape
    shp = (bsz, seq, DSA_HEADS, DSA_HEAD_DIM)
    q = rope(rms_norm(a_q.reshape(shp), qk_norm[0]), positions)
    k = rope(rms_norm(a_k.reshape(shp), qk_norm[1]), positions)
    v = a_v.reshape(shp)
    iq = rope(i_q.reshape(bsz, seq, IDX_HEADS, IDX_DIM), positions)
    ik = rope(i_k[:, :, None, :], positions)[:, :, 0]
    iw = i_w * IDX_HEADS ** -0.5
    out = dsa_sparse_attention(q, k, v, iq, ik, iw, k_sel)
    return out.reshape(bsz, seq, BRANCH_WIDTH)


def gla_chunked(q, k, v, log_a):
    bsz, seq, heads, dk = q.shape
    dv = v.shape[-1]
    n = seq // GLA_CHUNK

    def to_chunks(a):
        return a.reshape(bsz, n, GLA_CHUNK, heads, a.shape[-1]).transpose(1, 0, 3, 2, 4)

    qc, kc, vc = to_chunks(q), to_chunks(k), to_chunks(v)
    bc = jnp.cumsum(to_chunks(log_a), axis=3)
    causal = jnp.tril(jnp.ones((GLA_CHUNK, GLA_CHUNK), dtype=bool))

    def step(state, inp):
        qt, kt, vt, bt = inp
        diff = bt[:, :, :, None, :] - bt[:, :, None, :, :]
        decay = jnp.exp(jnp.where(causal[:, :, None], diff, -jnp.inf))
        attn = jnp.einsum('bhtd,bhsd,bhtsd->bhts', qt, kt, decay)
        o = jnp.einsum('bhts,bhsv->bhtv', attn, vt)
        o = o + jnp.einsum('bhtd,bhdv->bhtv', qt * jnp.exp(bt), state)
        b_last = bt[:, :, -1]
        k_dec = kt * jnp.exp(b_last[:, :, None, :] - bt)
        state = jnp.exp(b_last)[..., None] * state + jnp.einsum('bhsd,bhsv->bhdv', k_dec, vt)
        return state, o

    state0 = jnp.zeros((bsz, heads, dk, dv), jnp.float32)
    _, ys = lax.scan(step, state0, (qc, kc, vc, bc))
    return ys.transpose(1, 0, 3, 2, 4).reshape(bsz, seq, heads, dv)


def gla_branch(g_q, g_k, g_v, g_lr, g_r, w_gate2, b_gate2, out_norm):
    bsz, seq, _ = g_q.shape
    f32 = jnp.float32
    q = g_q.reshape(bsz, seq, GLA_HEADS, GLA_KEY_DIM).astype(f32) * GLA_KEY_DIM ** -0.5
    k = g_k.reshape(bsz, seq, GLA_HEADS, GLA_KEY_DIM).astype(f32)
    v = g_v.reshape(bsz, seq, GLA_HEADS, GLA_VAL_DIM).astype(f32)
    log_a = jax.nn.log_sigmoid((g_lr @ w_gate2 + b_gate2).astype(f32)) / GLA_TAU
    log_a = log_a.reshape(bsz, seq, GLA_HEADS, GLA_KEY_DIM)
    o = rms_norm(gla_chunked(q, k, v, log_a), out_norm)
    o = o.reshape(bsz, seq, BRANCH_WIDTH) * jax.nn.silu(g_r.astype(f32))
    return o.astype(g_q.dtype)


def _complex_linear_combine(e1, e2):
    a1r, a1i, b1r, b1i = e1
    a2r, a2i, b2r, b2i = e2
    ar = a1r * a2r - a1i * a2i
    ai = a1r * a2i + a1i * a2r
    br = a2r * b1r - a2i * b1i + b2r
    bi = a2r * b1i + a2i * b1r + b2i
    return ar, ai, br, bi


def s5_branch(u, a_re, a_im, log_dt, b_re, b_im, c_re, c_im, d_skip, w_glu, b_glu):
    bsz, seq, _ = u.shape
    f32 = jnp.float32
    uf = u.astype(f32).reshape(bsz, seq, S5_GROUPS, S5_GROUP)
    ar, ai = a_re.astype(f32), a_im.astype(f32)
    dt = jnp.exp(log_dt.astype(f32))[:, None]
    mag = jnp.exp(dt * ar)
    abar_re, abar_im = mag * jnp.cos(dt * ai), mag * jnp.sin(dt * ai)
    den = ar * ar + ai * ai
    nr, ni = abar_re - 1.0, abar_im
    z_re, z_im = (nr * ar + ni * ai) / den, (ni * ar - nr * ai) / den
    bu_re = jnp.einsum('bsgi,gpi->bsgp', uf, b_re.astype(f32))
    bu_im = jnp.einsum('bsgi,gpi->bsgp', uf, b_im.astype(f32))
    x_re = z_re * bu_re - z_im * bu_im
    x_im = z_re * bu_im + z_im * bu_re
    elems = (jnp.broadcast_to(abar_re, x_re.shape), jnp.broadcast_to(abar_im, x_re.shape), x_re, x_im)
    _, _, h_re, h_im = lax.associative_scan(_complex_linear_combine, elems, axis=1)
    y = jnp.einsum('bsgp,gip->bsgi', h_re, c_re.astype(f32)) - jnp.einsum('bsgp,gip->bsgi', h_im, c_im.astype(f32))
    y = y.reshape(bsz, seq, BRANCH_WIDTH) + d_skip.astype(f32) * uf.reshape(bsz, seq, BRANCH_WIDTH)
    y = jax.nn.gelu(y)
    y = y * jax.nn.sigmoid(y @ w_glu.astype(f32) + b_glu.astype(f32))
    return y.astype(u.dtype)


def dense_causal_attention(q, k, v):
    bsz, seq, heads, dqk = q.shape
    nb = seq // Q_BLOCK
    key_pos = jnp.arange(seq)

    def block(args):
        qb, t = args
        logits = jnp.einsum('bqhd,bshd->bhqs', qb, k).astype(jnp.float32) * dqk ** -0.5
        logits = jnp.where(key_pos[None, :] <= t[:, None], logits, -jnp.inf)
        p = jax.nn.softmax(logits, axis=-1).astype(v.dtype)
        return jnp.einsum('bhqs,bshv->bqhv', p, v)

    qb = q.reshape(bsz, nb, Q_BLOCK, heads, dqk).swapaxes(0, 1)
    out = lax.map(block, (qb, key_pos.reshape(nb, Q_BLOCK)))
    return out.swapaxes(0, 1).reshape(bsz, seq, heads, v.shape[-1])


def mla_branch(m_q, m_kv, m_kr, positions, q_lat_norm, kv_lat_norm, w_uq, w_ukv, qk_norm):
    bsz, seq, _ = m_q.shape
    q = (rms_norm(m_q, q_lat_norm) @ w_uq).reshape(bsz, seq, MLA_HEADS, MLA_NOPE_DIM + MLA_ROPE_DIM)
    kv = (rms_norm(m_kv, kv_lat_norm) @ w_ukv).reshape(bsz, seq, MLA_HEADS, MLA_NOPE_DIM + MLA_V_DIM)
    k_nope, v = kv[..., :MLA_NOPE_DIM], kv[..., MLA_NOPE_DIM:]
    q_nope = rms_norm(q[..., :MLA_NOPE_DIM], qk_norm[0, :MLA_NOPE_DIM])
    q_rope = rope(rms_norm(q[..., MLA_NOPE_DIM:], qk_norm[0, MLA_NOPE_DIM:]), positions)
    k_nope = rms_norm(k_nope, qk_norm[1, :MLA_NOPE_DIM])
    k_rope = rope(rms_norm(m_kr, qk_norm[1, MLA_NOPE_DIM:])[:, :, None, :], positions)
    q = jnp.concatenate([q_nope, q_rope], axis=-1)
    k = jnp.concatenate([k_nope, jnp.broadcast_to(k_rope, (bsz, seq, MLA_HEADS, MLA_ROPE_DIM))], axis=-1)
    out = dense_causal_attention(q, k, v)
    return out.reshape(bsz, seq, MLA_HEADS * MLA_V_DIM)


def setup_inputs(seed: int = 0) -> dict:
    key = jax.random.key(seed)
    ks = iter(jax.random.split(key, 40))
    f32 = jnp.float32
    L, D = DEPTH, D_MODEL
    G, P = S5_GROUPS, S5_STATE

    def nrm(shape, scale):
        return jax.random.normal(next(ks), shape, f32) * scale

    inp = {}
    inp['x'] = nrm((BATCH, SEQ, D), 1.0)
    inp['c'] = nrm((BATCH, D), 1.0)
    inp['positions'] = jnp.tile(jnp.arange(SEQ, dtype=jnp.int32)[None, :], (BATCH, 1))
    inp['w_ada'] = nrm((L, D, N_MOD * D), D ** -0.5)
    inp['b_ada'] = nrm((L, N_MOD * D), 0.02)
    inp['norm_g'] = 1.0 + nrm((L, 3, D), 0.02)
    inp['w_ff1_in'] = nrm((L, D, 2 * D_FF), D ** -0.5)
    inp['w_ff1_out'] = nrm((L, D_FF, D), D_FF ** -0.5)
    inp['w_ff2_in'] = nrm((L, D, 2 * D_FF), D ** -0.5)
    inp['w_ff2_out'] = nrm((L, D_FF, D), D_FF ** -0.5)
    inp['w_in'] = nrm((L, D, IN_COLS), D ** -0.5)
    inp['dsa_qk_norm'] = 1.0 + nrm((L, 2, DSA_HEAD_DIM), 0.02)
    inp['gla_w_gate'] = nrm((L, GLA_GATE_RANK, GLA_HEADS * GLA_KEY_DIM), GLA_GATE_RANK ** -0.5)
    inp['gla_b_gate'] = nrm((L, GLA_HEADS * GLA_KEY_DIM), 0.1)
    inp['gla_out_norm'] = 1.0 + nrm((L, GLA_VAL_DIM), 0.02)
    n_idx = jnp.arange(P, dtype=f32)
    inp['s5_a_re'] = -0.5 + nrm((L, G, P), 0.01)
    inp['s5_a_im'] = jnp.pi * n_idx + nrm((L, G, P), 0.01)
    inp['s5_log_dt'] = jax.random.uniform(next(ks), (L, G), f32, math.log(S5_DT_MIN), math.log(S5_DT_MAX))
    inp['s5_b_re'] = nrm((L, G, P, S5_GROUP), (2 * S5_GROUP) ** -0.5)
    inp['s5_b_im'] = nrm((L, G, P, S5_GROUP), (2 * S5_GROUP) ** -0.5)
    inp['s5_c_re'] = nrm((L, G, S5_GROUP, P), P ** -0.5)
    inp['s5_c_im'] = nrm((L, G, S5_GROUP, P), P ** -0.5)
    inp['s5_d'] = nrm((L, BRANCH_WIDTH), 0.5)
    inp['s5_w_glu'] = nrm((L, BRANCH_WIDTH, BRANCH_WIDTH), BRANCH_WIDTH ** -0.5)
    inp['s5_b_glu'] = nrm((L, BRANCH_WIDTH), 0.02)
    inp['mla_q_norm'] = 1.0 + nrm((L, MLA_Q_RANK), 0.02)
    inp['mla_kv_norm'] = 1.0 + nrm((L, MLA_KV_RANK), 0.02)
    inp['mla_w_uq'] = nrm((L, MLA_Q_RANK, MLA_HEADS * (MLA_NOPE_DIM + MLA_ROPE_DIM)), MLA_Q_RANK ** -0.5)
    inp['mla_w_ukv'] = nrm((L, MLA_KV_RANK, MLA_HEADS * (MLA_NOPE_DIM + MLA_V_DIM)), MLA_KV_RANK ** -0.5)
    inp['mla_qk_norm'] = 1.0 + nrm((L, 2, MLA_NOPE_DIM + MLA_ROPE_DIM), 0.02)
    inp['w_branch'] = nrm((L, N_BRANCHES, BRANCH_WIDTH, D), BRANCH_WIDTH ** -0.5)
    inp['w_gate'] = nrm((L, N_BRANCHES, D, D), D ** -0.5)
    inp['w_out'] = nrm((L, D, D), D ** -0.5)
    return inp


def reference(x, c, positions, w_ada, b_ada, norm_g, w_ff1_in, w_ff1_out, w_ff2_in, w_ff2_out,
              w_in, dsa_qk_norm, gla_w_gate, gla_b_gate, gla_out_norm,
              s5_a_re, s5_a_im, s5_log_dt, s5_b_re, s5_b_im, s5_c_re, s5_c_im, s5_d, s5_w_glu, s5_b_glu,
              mla_q_norm, mla_kv_norm, mla_w_uq, mla_w_ukv, mla_qk_norm,
              w_branch, w_gate, w_out):
    bsz, seq, _ = x.shape
    k_sel = min(INDEX_TOPK, seq // 4)
    cond = jax.nn.silu(c)
    for l in range(DEPTH):
        mod = (cond @ w_ada[l] + b_ada[l]).reshape(bsz, N_MOD, 1, D_MODEL).astype(x.dtype)
        h = ada_norm(x, norm_g[l, 0], mod[:, 0], mod[:, 1])
        x = x + 0.5 * mod[:, 2] * swiglu(h, w_ff1_in[l], w_ff1_out[l])
        h = ada_norm(x, norm_g[l, 1], mod[:, 3], mod[:, 4])
        (a_q, a_k, a_v, i_q, i_k, i_w, g_q, g_k, g_v, g_lr, g_r, s_u, m_q, m_kv, m_kr) = jnp.split(
            h @ w_in[l], IN_OFFSETS, axis=-1)
        br_a = dsa_branch(a_q, a_k, a_v, i_q, i_k, i_w, positions, dsa_qk_norm[l], k_sel)
        br_b = gla_branch(g_q, g_k, g_v, g_lr, g_r, gla_w_gate[l], gla_b_gate[l], gla_out_norm[l])
        br_c = s5_branch(s_u, s5_a_re[l], s5_a_im[l], s5_log_dt[l], s5_b_re[l], s5_b_im[l],
                         s5_c_re[l], s5_c_im[l], s5_d[l], s5_w_glu[l], s5_b_glu[l])
        br_d = mla_branch(m_q, m_kv, m_kr, positions, mla_q_norm[l], mla_kv_norm[l],
                          mla_w_uq[l], mla_w_ukv[l], mla_qk_norm[l])
        merged = jax.nn.sigmoid(h @ w_gate[l, 0]) * (br_a @ w_branch[l, 0])
        merged = merged + jax.nn.sigmoid(h @ w_gate[l, 1]) * (br_b @ w_branch[l, 1])
        merged = merged + jax.nn.sigmoid(h @ w_gate[l, 2]) * (br_c @ w_branch[l, 2])
        merged = merged + jax.nn.sigmoid(h @ w_gate[l, 3]) * (br_d @ w_branch[l, 3])
        x = x + mod[:, 5] * (merged @ w_out[l])
        h = ada_norm(x, norm_g[l, 2], mod[:, 6], mod[:, 7])
        x = x + 0.5 * mod[:, 8] * swiglu(h, w_ff2_in[l], w_ff2_out[l])
    return x
```

```python
import functools
import math

import numpy as np
import jax
import jax.numpy as jnp
from jax import lax
from jax.experimental import pallas as pl
from jax.experimental.pallas import tpu as pltpu

F32 = jnp.float32
BF16 = jnp.bfloat16
I32 = jnp.int32

N_BRANCHES = 4
DSA_HEADS = 4
DSA_HEAD_DIM = 128
IDX_HEADS = 8
IDX_DIM = 64
INDEX_TOPK = 256
GLA_HEADS = 4
GLA_KEY_DIM = 64
GLA_VAL_DIM = 128
GLA_GATE_RANK = 16
GLA_TAU = 16.0
GLA_CHUNK = 64
GLA_SUB = 16
S5_GROUP = 16
S5_STATE = 64
S5_CHUNK = 64
MLA_HEADS = 4
MLA_NOPE_DIM = 128
MLA_ROPE_DIM = 64
MLA_V_DIM = 128
MLA_Q_RANK = 384
MLA_KV_RANK = 128
ROPE_THETA = 10000.0
NORM_EPS = 1e-6
N_MOD = 9

LANES = 128
FF_TILE = 512
ROW_TILE = 512
ATT_TILE = 256
INT_MIN = -(2 ** 31)
NEG_BIG = -1e30

P_AQ, P_AK, P_AV, P_IQ = 0, 512, 1024, 1536
P_GQ, P_GK, P_GV, P_GR = 2048, 2304, 2560, 3072
P_SU, P_MQKV, P_IKR, P_MISC = 3584, 4096, 4608, 4736
P_COLS = 4864
MISC_IW, MISC_GLR = 0, 8


def _cparams(sem, vmem_mb):
    return pltpu.CompilerParams(dimension_semantics=sem, vmem_limit_bytes=vmem_mb << 20)


def _dot(a, b, precision=None):
    return jnp.dot(a, b, preferred_element_type=F32, precision=precision)


def _dot_nt(a, b):
    return lax.dot_general(a, b, (((1,), (1,)), ((), ())), preferred_element_type=F32)


def _dot_tn(a, b):
    return lax.dot_general(a, b, (((0,), (0,)), ((), ())), preferred_element_type=F32)


def _rms(x, dim):
    return x * lax.rsqrt(jnp.sum(x * x, axis=-1, keepdims=True) * (1.0 / dim) + NORM_EPS)


def _ada_kernel(c_ref, w_ref, b_ref, o_ref):
    c = c_ref[...]
    cond = (c * jax.nn.sigmoid(c)).astype(BF16)
    o_ref[0] = _dot(cond, w_ref[0].astype(BF16)) + b_ref[0]


def _ada_all(c, w_ada, b_ada):
    depth, d, n = w_ada.shape
    bsz = c.shape[0]
    tn = 1024
    assert n % tn == 0
    return pl.pallas_call(
        _ada_kernel,
        grid=(depth, n // tn),
        in_specs=[
            pl.BlockSpec((bsz, d), lambda l, j: (0, 0)),
            pl.BlockSpec((1, d, tn), lambda l, j: (l, 0, j)),
            pl.BlockSpec((1, 1, tn), lambda l, j: (l, 0, j)),
        ],
        out_specs=pl.BlockSpec((1, bsz, tn), lambda l, j: (l, 0, j)),
        out_shape=jax.ShapeDtypeStruct((depth, bsz, n), F32),
        compiler_params=_cparams(("parallel", "parallel"), 48),
    )(c, w_ada, b_ada.reshape(depth, 1, n))


def _ada_norm_tile(x, gain, shift, scale):
    d = x.shape[-1]
    return _rms(x, d) * gain * (1.0 + scale) + shift


def _ffn_kernel(x_ref, mod_ref, g_ref, wa_ref, wb_ref, wo_ref, o_ref, h_scr, acc_scr, *, k0):
    f = pl.program_id(1)

    @pl.when(f == 0)
    def _():
        h = _ada_norm_tile(x_ref[...], g_ref[...], mod_ref[0, k0:k0 + 1, :], mod_ref[0, k0 + 1:k0 + 2, :])
        h_scr[...] = h.astype(BF16)
        acc_scr[...] = jnp.zeros_like(acc_scr)

    h = h_scr[...]
    a = _dot(h, wa_ref[...])
    b = _dot(h, wb_ref[...])
    u = (a * jax.nn.sigmoid(a)) * b
    acc_scr[...] += _dot(u.astype(BF16), wo_ref[...])

    @pl.when(f == pl.num_programs(1) - 1)
    def _():
        o_ref[...] = x_ref[...] + (0.5 * mod_ref[0, k0 + 2:k0 + 3, :]) * acc_scr[...]


def _ffn(x2, mod_l, gain, w_in_p, w_out_p, seq, k0):
    m, d = x2.shape
    ffp = w_out_p.shape[0]
    tm, tf = ROW_TILE, FF_TILE
    nf = ffp // tf
    tiles_per_batch = seq // tm
    return pl.pallas_call(
        functools.partial(_ffn_kernel, k0=k0),
        grid=(m // tm, nf),
        in_specs=[
            pl.BlockSpec((tm, d), lambda i, f: (i, 0)),
            pl.BlockSpec((1, N_MOD, d), lambda i, f: (i // tiles_per_batch, 0, 0)),
            pl.BlockSpec((1, d), lambda i, f: (0, 0)),
            pl.BlockSpec((d, tf), lambda i, f: (0, f)),
            pl.BlockSpec((d, tf), lambda i, f: (0, f + nf)),
            pl.BlockSpec((tf, d), lambda i, f: (f, 0)),
        ],
        out_specs=pl.BlockSpec((tm, d), lambda i, f: (i, 0)),
        out_shape=jax.ShapeDtypeStruct((m, d), F32),
        scratch_shapes=[pltpu.VMEM((tm, d), BF16), pltpu.VMEM((tm, d), F32)],
        compiler_params=_cparams(("parallel", "arbitrary"), 56),
    )(x2, mod_l, gain.reshape(1, d), w_in_p, w_in_p, w_out_p)


def _inproj_kernel(x_ref, mod_ref, g_ref, w_ref, h_ref, p_ref):
    @pl.when(pl.program_id(1) == 0)
    def _():
        h = _ada_norm_tile(x_ref[...], g_ref[...], mod_ref[0, 3:4, :], mod_ref[0, 4:5, :])
        h_ref[...] = h.astype(BF16)

    p_ref[...] = _dot(h_ref[...], w_ref[...])


def _inproj(x2, mod_l, gain, w_in_p, seq):
    m, d = x2.shape
    n = w_in_p.shape[1]
    tm = ROW_TILE
    tn = n // 2
    tiles_per_batch = seq // tm
    return pl.pallas_call(
        _inproj_kernel,
        grid=(m // tm, n // tn),
        in_specs=[
            pl.BlockSpec((tm, d), lambda i, j: (i, 0)),
            pl.BlockSpec((1, N_MOD, d), lambda i, j: (i // tiles_per_batch, 0, 0)),
            pl.BlockSpec((1, d), lambda i, j: (0, 0)),
            pl.BlockSpec((d, tn), lambda i, j: (0, j)),
        ],
        out_specs=[
            pl.BlockSpec((tm, d), lambda i, j: (i, 0)),
            pl.BlockSpec((tm, tn), lambda i, j: (i, j)),
        ],
        out_shape=[jax.ShapeDtypeStruct((m, d), BF16), jax.ShapeDtypeStruct((m, n), F32)],
        compiler_params=_cparams(("parallel", "arbitrary"), 56),
    )(x2, mod_l, gain.reshape(1, d), w_in_p)


def _rope128(x, cos, sin):
    return x * cos + pltpu.roll(x, 64, 1) * sin


def _rope64(x, cos, sin, lane):
    rot = jnp.where((lane & 63) < 32, pltpu.roll(x, 96, 1), pltpu.roll(x, 32, 1))
    return x * cos + rot * sin


def _prep_kernel(aq_ref, ak_ref, av_ref, iq_ref, mqkv_ref, ikr_ref,
                 c128_ref, s128_ref, c64_ref, s64_ref,
                 dsan_ref, mqn_ref, mkvn_ref, wuq_ref, wukn_ref, wuv_ref, mqkn_ref,
                 qa_o, ka_o, va_o, iq_o, ik2_o, qm_o, km_o, vm_o):
    tm = aq_ref.shape[0]
    lane = lax.broadcasted_iota(I32, (tm, LANES), 1)
    low = lane < 64
    c128, s128 = c128_ref[...], s128_ref[...]
    c64, s64 = c64_ref[...], s64_ref[...]

    aq, ak = aq_ref[...], ak_ref[...]
    for h in range(DSA_HEADS):
        sl = slice(h * LANES, (h + 1) * LANES)
        qh = _rope128(_rms(aq[:, sl], DSA_HEAD_DIM) * dsan_ref[0:1, :], c128, s128)
        kh = _rope128(_rms(ak[:, sl], DSA_HEAD_DIM) * dsan_ref[1:2, :], c128, s128)
        qa_o[:, sl] = (qh * DSA_HEAD_DIM ** -0.5).astype(BF16)
        ka_o[:, sl] = kh.astype(BF16)
    va_o[...] = av_ref[...].astype(BF16)

    iq = iq_ref[...]
    for blk in range(IDX_HEADS * IDX_DIM // LANES):
        sl = slice(blk * LANES, (blk + 1) * LANES)
        iq_o[:, sl] = _rope64(iq[:, sl], c64, s64, lane).astype(BF16)

    x = ikr_ref[...]
    ms = jnp.sum(jnp.where(low, 0.0, x * x), axis=-1, keepdims=True) * (1.0 / MLA_ROPE_DIM)
    xn = jnp.where(low, x, x * lax.rsqrt(ms + NORM_EPS) * mqkn_ref[3:4, :])
    xr = _rope64(xn, c64, s64, lane)
    xs = pltpu.roll(xr, 64, 1)
    ik2_o[:, 0:LANES] = jnp.where(low, xr, 0.0).astype(BF16)
    ik2_o[:, LANES:2 * LANES] = jnp.where(low, 0.0, xs).astype(BF16)
    kr128 = jnp.where(low, xs, 0.0).astype(BF16)

    mqkv = mqkv_ref[...]
    ql = (_rms(mqkv[:, 0:MLA_Q_RANK], MLA_Q_RANK) * mqn_ref[...]).astype(BF16)
    kvl = (_rms(mqkv[:, MLA_Q_RANK:MLA_Q_RANK + MLA_KV_RANK], MLA_KV_RANK) * mkvn_ref[...]).astype(BF16)
    qf = _dot(ql, wuq_ref[...])
    kn = _dot(kvl, wukn_ref[...])
    vm_o[...] = _dot(kvl, wuv_ref[...]).astype(BF16)
    qscale = (MLA_NOPE_DIM + MLA_ROPE_DIM) ** -0.5
    for h in range(MLA_HEADS):
        qn = _rms(qf[:, h * 256:h * 256 + LANES], MLA_NOPE_DIM) * mqkn_ref[0:1, :]
        qr_raw = qf[:, h * 256 + LANES:(h + 1) * 256]
        msr = jnp.sum(qr_raw * qr_raw, axis=-1, keepdims=True) * (1.0 / MLA_ROPE_DIM)
        qr = qr_raw * lax.rsqrt(msr + NORM_EPS) * mqkn_ref[1:2, :]
        qr = jnp.where(low, _rope64(qr, c64, s64, lane), 0.0)
        qm_o[:, h * 256:h * 256 + LANES] = (qn * qscale).astype(BF16)
        qm_o[:, h * 256 + LANES:(h + 1) * 256] = (qr * qscale).astype(BF16)
        kh = _rms(kn[:, h * LANES:(h + 1) * LANES], MLA_NOPE_DIM) * mqkn_ref[2:3, :]
        km_o[:, h * 256:h * 256 + LANES] = kh.astype(BF16)
        km_o[:, h * 256 + LANES:(h + 1) * 256] = kr128


def _prep(p2, tabs, lw):
    m = p2.shape[0]
    tm = ROW_TILE

    def col(width, off):
        assert off % width == 0
        return pl.BlockSpec((tm, width), lambda i: (i, off // width))

    def full(a):
        return pl.BlockSpec(a.shape, lambda i: (0,) * a.ndim)

    tab = pl.BlockSpec((tm, LANES), lambda i: (i, 0))
    consts = [lw['dsa_norm'], lw['mla_q_norm'], lw['mla_kv_norm'], lw['w_uq'], lw['w_ukn'], lw['w_uv'],
              lw['mla_qk_norm']]
    outs = [(512, BF16)] * 4 + [(256, BF16), (1024, BF16), (1024, BF16), (512, BF16)]
    return pl.pallas_call(
        _prep_kernel,
        grid=(m // tm,),
        in_specs=[col(512, P_AQ), col(512, P_AK), col(512, P_AV), col(512, P_IQ), col(512, P_MQKV),
                  col(LANES, P_IKR), tab, tab, tab, tab] + [full(a) for a in consts],
        out_specs=[pl.BlockSpec((tm, w), lambda i: (i, 0)) for w, _ in outs],
        out_shape=[jax.ShapeDtypeStruct((m, w), dt) for w, dt in outs],
        compiler_params=_cparams(("parallel",), 48),
    )(p2, p2, p2, p2, p2, p2, *tabs, *consts)


def _softmax_tile(s, m, l, acc, v):
    mn = jnp.maximum(m, jnp.max(s, axis=-1, keepdims=True))
    alpha = jnp.exp(m - mn)
    p = jnp.exp(s - mn)
    l = alpha * l + jnp.sum(p, axis=-1, keepdims=True)
    acc = alpha * acc + _dot(p.astype(BF16), v)
    return mn, l, acc


def _softmax_init(heads, tq, dv):
    return (tuple(jnp.full((tq, 1), NEG_BIG, F32) for _ in range(heads)),
            tuple(jnp.zeros((tq, 1), F32) for _ in range(heads)),
            tuple(jnp.zeros((tq, dv), F32) for _ in range(heads)))


def _dsa_kernel(q_ref, iq_ref, misc_ref, k_ref, v_ref, ik2_ref, o_ref, key_scr, *, k_sel, idx_bits):
    tq = q_ref.shape[1]
    qi = pl.program_id(1)
    nt = qi + 1
    kf = float(k_sel)

    iq = iq_ref[0]
    w = misc_ref[...] * (IDX_HEADS ** -0.5 * IDX_DIM ** -0.5)
    wcols = [w[:, MISC_IW + h:MISC_IW + h + 1] for h in range(IDX_HEADS)]
    row = lax.broadcasted_iota(I32, (tq, tq), 0)
    colv = lax.broadcasted_iota(I32, (tq, tq), 1)

    def score_tile(j, carry):
        ks = pl.multiple_of(j * tq, tq)
        ikl = ik2_ref[0, pl.ds(ks, tq), 0:LANES]
        ikh = ik2_ref[0, pl.ds(ks, tq), LANES:2 * LANES]
        acc = jnp.zeros((tq, tq), F32)
        for hp in range(IDX_HEADS // 2):
            a = iq[:, hp * LANES:(hp + 1) * LANES]
            acc = acc + wcols[2 * hp] * jnp.maximum(_dot_nt(a, ikl), 0.0)
            acc = acc + wcols[2 * hp + 1] * jnp.maximum(_dot_nt(a, ikh), 0.0)
        acc = jnp.where(acc == 0.0, 0.0, acc)
        bits = pltpu.bitcast(acc, I32)
        key = bits ^ ((bits >> 31) & 0x7FFFFFFF)
        key = jnp.where(colv + (j - qi) * tq <= row, key, INT_MIN)
        key_scr[j] = key
        return carry

    lax.fori_loop(0, nt, score_tile, 0)

    def lane_fold(mask_f):
        out = mask_f[:, 0:LANES]
        for t in range(1, tq // LANES):
            out = out + mask_f[:, t * LANES:(t + 1) * LANES]
        return out

    def count(pred):
        def body(j, acc):
            return acc + lane_fold(jnp.where(pred(key_scr[j], j), 1.0, 0.0))
        acc = lax.fori_loop(0, nt, body, jnp.zeros((tq, LANES), F32))
        return jnp.sum(acc, axis=-1, keepdims=True)

    c0 = count(lambda key, j: key >= 0)
    lo = jnp.where(c0 >= kf, 0, INT_MIN).astype(I32)

    def bs_body(i, lo):
        cand = lo | jnp.left_shift(jnp.int32(1), 30 - i)
        c = count(lambda key, j: key >= cand)
        return jnp.where(c >= kf, cand, lo)

    thr = lax.fori_loop(0, 31, bs_body, lo)

    c_ge = count(lambda key, j: key >= thr)
    excess = jnp.where((c_ge > kf) & (thr > INT_MIN), 1.0, 0.0)

    @pl.when(jnp.max(excess) > 0.0)
    def _():
        need = kf - count(lambda key, j: key > thr)

        def tie_lt(p):
            return count(lambda key, j: (key == thr) & (colv + j * tq < p))

        def ib_body(i, p):
            cand = p | jnp.left_shift(jnp.int32(1), idx_bits - 1 - i)
            return jnp.where(tie_lt(cand) < need, cand, p)

        last = lax.fori_loop(0, idx_bits, ib_body, jnp.zeros((tq, 1), I32))

        def demote(j, carry):
            key = key_scr[j]
            key_scr[j] = jnp.where((key == thr) & (colv + j * tq > last), INT_MIN, key)
            return carry

        lax.fori_loop(0, nt, demote, 0)

    thr_eff = jnp.maximum(thr, INT_MIN + 1)
    q = q_ref[0]

    def att_tile(j, carry):
        m, l, acc = carry
        ks = pl.multiple_of(j * tq, tq)
        sel = key_scr[j] >= thr_eff
        kj = k_ref[0, pl.ds(ks, tq), :]
        vj = v_ref[0, pl.ds(ks, tq), :]
        mo, lo_, ao = [], [], []
        for h in range(DSA_HEADS):
            sl = slice(h * LANES, (h + 1) * LANES)
            s = jnp.where(sel, _dot_nt(q[:, sl], kj[:, sl]), NEG_BIG)
            mh, lh, ah = _softmax_tile(s, m[h], l[h], acc[h], vj[:, sl])
            mo.append(mh), lo_.append(lh), ao.append(ah)
        return tuple(mo), tuple(lo_), tuple(ao)

    m, l, acc = lax.fori_loop(0, nt, att_tile, _softmax_init(DSA_HEADS, tq, DSA_HEAD_DIM))
    for h in range(DSA_HEADS):
        o_ref[0, :, h * LANES:(h + 1) * LANES] = (acc[h] / l[h]).astype(BF16)


def _dsa(qa, iq, p2, ka, va, ik2, bsz, seq, k_sel):
    tq = ATT_TILE
    assert k_sel <= tq and seq % tq == 0
    nq = seq // tq
    hw = DSA_HEADS * DSA_HEAD_DIM
    r3 = lambda a: a.reshape(bsz, seq, a.shape[-1])
    return pl.pallas_call(
        functools.partial(_dsa_kernel, k_sel=k_sel, idx_bits=int(math.ceil(math.log2(seq)))),
        grid=(bsz, nq),
        in_specs=[
            pl.BlockSpec((1, tq, hw), lambda b, i: (b, i, 0)),
            pl.BlockSpec((1, tq, IDX_HEADS * IDX_DIM), lambda b, i: (b, i, 0)),
            pl.BlockSpec((tq, LANES), lambda b, i: (b * nq + i, P_MISC // LANES)),
            pl.BlockSpec((1, seq, hw), lambda b, i: (b, 0, 0)),
            pl.BlockSpec((1, seq, hw), lambda b, i: (b, 0, 0)),
            pl.BlockSpec((1, seq, 2 * LANES), lambda b, i: (b, 0, 0)),
        ],
        out_specs=pl.BlockSpec((1, tq, hw), lambda b, i: (b, i, 0)),
        out_shape=jax.ShapeDtypeStruct((bsz, seq, hw), BF16),
        scratch_shapes=[pltpu.VMEM((nq, tq, tq), I32)],
        compiler_params=_cparams(("parallel", "arbitrary"), 48),
    )(r3(qa), r3(iq), p2, r3(ka), r3(va), r3(ik2)).reshape(bsz * seq, hw)


def _mla_kernel(q_ref, k_ref, v_ref, o_ref):
    tq = q_ref.shape[1]
    qi = pl.program_id(1)
    q = q_ref[0]
    row = lax.broadcasted_iota(I32, (tq, tq), 0)
    colv = lax.broadcasted_iota(I32, (tq, tq), 1)

    def tile(j, carry, masked):
        m, l, acc = carry
        ks = pl.multiple_of(j * tq, tq)
        kj = k_ref[0, pl.ds(ks, tq), :]
        vj = v_ref[0, pl.ds(ks, tq), :]
        mo, lo_, ao = [], [], []
        for h in range(MLA_HEADS):
            s = _dot_nt(q[:, h * 256:(h + 1) * 256], kj[:, h * 256:(h + 1) * 256])
            if masked:
                s = jnp.where(colv <= row, s, NEG_BIG)
            mh, lh, ah = _softmax_tile(s, m[h], l[h], acc[h], vj[:, h * LANES:(h + 1) * LANES])
            mo.append(mh), lo_.append(lh), ao.append(ah)
        return tuple(mo), tuple(lo_), tuple(ao)

    carry = lax.fori_loop(0, qi, functools.partial(tile, masked=False),
                          _softmax_init(MLA_HEADS, tq, MLA_V_DIM))
    m, l, acc = tile(qi, carry, True)
    for h in range(MLA_HEADS):
        o_ref[0, :, h * LANES:(h + 1) * LANES] = (acc[h] / l[h]).astype(BF16)


def _mla(qm, km, vm, bsz, seq):
    tq = ATT_TILE
    nq = seq // tq
    r3 = lambda a: a.reshape(bsz, seq, a.shape[-1])
    qw, vw = qm.shape[-1], vm.shape[-1]
    return pl.pallas_call(
        _mla_kernel,
        grid=(bsz, nq),
        in_specs=[
            pl.BlockSpec((1, tq, qw), lambda b, i: (b, i, 0)),
            pl.BlockSpec((1, seq, qw), lambda b, i: (b, 0, 0)),
            pl.BlockSpec((1, seq, vw), lambda b, i: (b, 0, 0)),
        ],
        out_specs=pl.BlockSpec((1, tq, vw), lambda b, i: (b, i, 0)),
        out_shape=jax.ShapeDtypeStruct((bsz, seq, vw), BF16),
        compiler_params=_cparams(("parallel", "arbitrary"), 48),
    )(r3(qm), r3(km), r3(vm)).reshape(bsz * seq, vw)


def _gla_kernel(gq_ref, gk_ref, gv_ref, gr_ref, misc_ref, w2_ref, b2_ref, on_ref, ind_ref,
                o_ref, st_scr):
    ch, sub = GLA_CHUNK, GLA_SUB
    dk, dv = GLA_KEY_DIM, GLA_VAL_DIM
    hi = lax.Precision.HIGHEST

    @pl.when(pl.program_id(1) == 0)
    def _():
        st_scr[...] = jnp.zeros_like(st_scr)

    q = gq_ref[...] * dk ** -0.5
    k = gk_ref[...]
    v = gv_ref[...]
    vb = v.astype(BF16)
    x = _dot(misc_ref[...], w2_ref[...], hi) + b2_ref[...]
    log_a = (jnp.minimum(x, 0.0) - jnp.log1p(jnp.exp(-jnp.abs(x)))) * (1.0 / GLA_TAU)
    tri = (lax.broadcasted_iota(I32, (ch, ch), 0) >= lax.broadcasted_iota(I32, (ch, ch), 1)).astype(F32)
    b = _dot(tri, log_a, hi)

    tcol = lax.broadcasted_iota(I32, (sub, 1), 0)
    rows = []
    for i in range(ch // sub):
        r0 = i * sub
        qs, ks, bs, vs = q[r0:r0 + sub], k[r0:r0 + sub], b[r0:r0 + sub], v[r0:r0 + sub]
        ps = []
        for j in range(sub):
            e = jnp.exp(jnp.minimum(bs - bs[j:j + 1], 0.0))
            ps.append(jnp.where(tcol >= j, qs * (ks[j:j + 1] * e), 0.0))
        r = _dot(jnp.concatenate(ps, axis=0).astype(BF16), ind_ref[...])
        od = r[0:sub] * vs[0:1]
        for j in range(1, sub):
            od = od + r[j * sub:(j + 1) * sub] * vs[j:j + 1]
        if i > 0:
            ref_b = b[r0 - 1:r0]
            qd = (qs * jnp.exp(bs - ref_b)).astype(BF16)
            kd = (k[0:r0] * jnp.exp(ref_b - b[0:r0])).astype(BF16)
            parts = []
            for h in range(GLA_HEADS):
                a = _dot_nt(qd[:, h * dk:(h + 1) * dk], kd[:, h * dk:(h + 1) * dk])
                parts.append(_dot(a.astype(BF16), vb[0:r0, h * dv:(h + 1) * dv]))
            od = od + jnp.concatenate(parts, axis=1)
        rows.append(od)
    o = jnp.concatenate(rows, axis=0)

    qe = (q * jnp.exp(b)).astype(BF16)
    b_last = b[ch - 1:ch]
    kdec = (k * jnp.exp(b_last - b)).astype(BF16)
    gr = gr_ref[...]
    for h in range(GLA_HEADS):
        st = st_scr[h]
        oh = o[:, h * dv:(h + 1) * dv] + _dot_nt(qe[:, h * dk:(h + 1) * dk], st.astype(BF16))
        oh = _rms(oh, dv) * on_ref[...]
        g = gr[:, h * dv:(h + 1) * dv]
        o_ref[:, h * dv:(h + 1) * dv] = (oh * (g * jax.nn.sigmoid(g))).astype(BF16)
        st_scr[h] = st * jnp.exp(b_last[:, h * dk:(h + 1) * dk]) + _dot_tn(
            vb[:, h * dv:(h + 1) * dv], kdec[:, h * dk:(h + 1) * dk])


def _gla(p2, lw, bsz, seq):
    ch = GLA_CHUNK
    nc = seq // ch
    hk, hv = GLA_HEADS * GLA_KEY_DIM, GLA_HEADS * GLA_VAL_DIM

    def col(width, off):
        assert off % width == 0
        return pl.BlockSpec((ch, width), lambda b, c: (b * nc + c, off // width))

    def full(a):
        return pl.BlockSpec(a.shape, lambda b, c: (0,) * a.ndim)

    consts = [lw['gla_w2'], lw['gla_b2'], lw['gla_on'], lw['gla_ind']]
    return pl.pallas_call(
        _gla_kernel,
        grid=(bsz, nc),
        in_specs=[col(hk, P_GQ), col(hk, P_GK), col(hv, P_GV), col(hv, P_GR), col(LANES, P_MISC)]
        + [full(a) for a in consts],
        out_specs=pl.BlockSpec((ch, hv), lambda b, c: (b * nc + c, 0)),
        out_shape=jax.ShapeDtypeStruct((bsz * seq, hv), BF16),
        scratch_shapes=[pltpu.VMEM((GLA_HEADS, GLA_VAL_DIM, GLA_KEY_DIM), F32)],
        compiler_params=_cparams(("parallel", "arbitrary"), 32),
    )(p2, p2, p2, p2, p2, *consts)


def _s5_scan_kernel(u_ref, kt_ref, w_ref, v_ref, dec_ref, y_ref, l_scr, hin_scr, *, bsz):
    u = u_ref[0]
    l_scr[...] = _dot(u, w_ref[0])
    n = u.shape[0] // bsz
    ar, ai = dec_ref[0, 0:1, :], dec_ref[0, 1:2, :]

    def step(c, h):
        r = pl.multiple_of(c * bsz, bsz)
        hin_scr[pl.ds(r, bsz), :] = h
        return ar * h + ai * pltpu.roll(h, S5_STATE, 1) + l_scr[pl.ds(r, bsz), :]

    lax.fori_loop(0, n, step, jnp.zeros((bsz, 2 * S5_STATE), F32))
    y_ref[0] = _dot(u, kt_ref[0]) + _dot(hin_scr[...].astype(BF16), v_ref[0])


def _s5_scan(u_g, lw, bsz):
    g, rows, width = u_g.shape
    st2 = 2 * S5_STATE
    blk = lambda a: pl.BlockSpec((1,) + a.shape[1:], lambda i: (i,) + (0,) * (a.ndim - 1))
    args = [u_g, lw['s5_kt'], lw['s5_w'], lw['s5_v'], lw['s5_dec']]
    return pl.pallas_call(
        functools.partial(_s5_scan_kernel, bsz=bsz),
        grid=(g,),
        in_specs=[blk(a) for a in args],
        out_specs=pl.BlockSpec((1, rows, width), lambda i: (i, 0, 0)),
        out_shape=jax.ShapeDtypeStruct((g, rows, width), F32),
        scratch_shapes=[pltpu.VMEM((rows, st2), F32), pltpu.VMEM((rows, st2), F32)],
        compiler_params=_cparams(("parallel",), 32),
    )(*args)


def _s5_out_kernel(y_ref, u_ref, d_ref, w_ref, b_ref, o_ref):
    y = y_ref[...] + d_ref[...] * u_ref[...]
    y = 0.5 * y * (1.0 + jnp.tanh(math.sqrt(2.0 / math.pi) * (y + 0.044715 * (y * y * y))))
    gate = jax.nn.sigmoid(_dot(y.astype(BF16), w_ref[...]) + b_ref[...])
    o_ref[...] = (y * gate).astype(BF16)


def _s5_out(y2, p2, lw):
    m, w = y2.shape
    tm = ROW_TILE
    full = lambda a: pl.BlockSpec(a.shape, lambda i: (0,) * a.ndim)
    consts = [lw['s5_d'], lw['s5_w_glu'], lw['s5_b_glu']]
    return pl.pallas_call(
        _s5_out_kernel,
        grid=(m // tm,),
        in_specs=[pl.BlockSpec((tm, w), lambda i: (i, 0)), pl.BlockSpec((tm, w), lambda i: (i, P_SU // w))]
        + [full(a) for a in consts],
        out_specs=pl.BlockSpec((tm, w), lambda i: (i, 0)),
        out_shape=jax.ShapeDtypeStruct((m, w), BF16),
        compiler_params=_cparams(("parallel",), 32),
    )(y2, p2, *consts)


def _s5(p2, lw, bsz, seq):
    t = S5_CHUNK
    bw = lw['s5_d'].shape[-1]
    g = bw // S5_GROUP
    nc = seq // t
    u = p2[:, P_SU:P_SU + bw].astype(BF16).reshape(bsz, nc, t, g, S5_GROUP)
    u_g = u.transpose(3, 1, 0, 2, 4).reshape(g, nc * bsz, t * S5_GROUP)
    y_g = _s5_scan(u_g, lw, bsz)
    y2 = y_g.reshape(g, nc, bsz, t, S5_GROUP).transpose(2, 1, 3, 0, 4).reshape(bsz * seq, bw)
    return _s5_out(y2, p2, lw)


def _merge_kernel(h_ref, ba_ref, bb_ref, bc_ref, bd_ref, wg_ref, wb_ref, o_ref):
    h = h_ref[...]
    acc = None
    for i, br in enumerate((ba_ref, bb_ref, bc_ref, bd_ref)):
        t = jax.nn.sigmoid(_dot(h, wg_ref[i])) * _dot(br[...], wb_ref[i])
        acc = t if acc is None else acc + t
    o_ref[...] = acc.astype(BF16)


def _merge(h2, branches, w_gate, w_branch):
    m, d = h2.shape
    bw = w_branch.shape[1]
    tm, tn = ROW_TILE, 512
    row = lambda w: pl.BlockSpec((tm, w), lambda j, i: (i, 0))
    return pl.pallas_call(
        _merge_kernel,
        grid=(d // tn, m // tm),
        in_specs=[row(d), row(bw), row(bw), row(bw), row(bw),
                  pl.BlockSpec((N_BRANCHES, d, tn), lambda j, i: (0, 0, j)),
                  pl.BlockSpec((N_BRANCHES, bw, tn), lambda j, i: (0, 0, j))],
        out_specs=pl.BlockSpec((tm, tn), lambda j, i: (i, j)),
        out_shape=jax.ShapeDtypeStruct((m, d), BF16),
        compiler_params=_cparams(("parallel", "parallel"), 56),
    )(h2, *branches, w_gate, w_branch)


def _outproj_kernel(x_ref, mod_ref, a_ref, w_ref, o_ref):
    o_ref[...] = x_ref[...] + mod_ref[0, 5:6, :] * _dot(a_ref[...], w_ref[...])


def _outproj(x2, mod_l, merged, w_out, seq):
    m, d = x2.shape
    tm = ROW_TILE
    tiles_per_batch = seq // tm
    return pl.pallas_call(
        _outproj_kernel,
        grid=(m // tm,),
        in_specs=[
            pl.BlockSpec((tm, d), lambda i: (i, 0)),
            pl.BlockSpec((1, N_MOD, d), lambda i: (i // tiles_per_batch, 0, 0)),
            pl.BlockSpec((tm, d), lambda i: (i, 0)),
            pl.BlockSpec((d, d), lambda i: (0, 0)),
        ],
        out_specs=pl.BlockSpec((tm, d), lambda i: (i, 0)),
        out_shape=jax.ShapeDtypeStruct((m, d), F32),
        compiler_params=_cparams(("parallel",), 48),
    )(x2, mod_l, merged, w_out)


def _pad_cols(w, n):
    return jnp.pad(w, ((0, 0), (0, n - w.shape[1])))


def _prep_ffn_weights(w_in, w_out):
    d_ff = w_out.shape[0]
    ffp = -(-d_ff // FF_TILE) * FF_TILE
    w_in_p = jnp.concatenate([_pad_cols(w_in[:, :d_ff], ffp), _pad_cols(w_in[:, d_ff:], ffp)], axis=1)
    w_out_p = jnp.pad(w_out, ((0, ffp - d_ff), (0, 0)))
    return w_in_p.astype(BF16), w_out_p.astype(BF16)


def _prep_w_in(w_in):
    sizes = (512, 512, 512, 512, 64, 8, 256, 256, 512, 16, 512, 512, 384, 128, 64)
    offs = np.concatenate([[0], np.cumsum(sizes)])
    seg = lambda i: w_in[:, offs[i]:offs[i + 1]]
    (a_q, a_k, a_v, i_q, i_k, i_w, g_q, g_k, g_v, g_lr, g_r, s_u, m_q, m_kv, m_kr) = [seg(i) for i in range(15)]
    pad = jnp.zeros((w_in.shape[0], P_COLS - P_MISC - 24), w_in.dtype)
    out = jnp.concatenate([a_q, a_k, a_v, i_q, g_q, g_k, g_v, g_r, s_u, m_q, m_kv, i_k, m_kr, i_w, g_lr, pad],
                          axis=1)
    assert out.shape[1] == P_COLS
    return out.astype(BF16)


def _prep_s5(a_re, a_im, log_dt, b_re, b_im, c_re, c_im):
    t = S5_CHUNK
    g, p = a_re.shape
    n_in = b_re.shape[-1]
    dt = jnp.exp(log_dt)[:, None]
    mag = jnp.exp(dt * a_re)
    abar_re, abar_im = mag * jnp.cos(dt * a_im), mag * jnp.sin(dt * a_im)
    den = a_re * a_re + a_im * a_im
    nr, ni = abar_re - 1.0, abar_im
    z_re, z_im = (nr * a_re + ni * a_im) / den, (ni * a_re - nr * a_im) / den
    j = jnp.arange(t + 1, dtype=F32)[None, :, None]
    pmag = jnp.exp(j * (dt * a_re)[:, None, :])
    pw_re = pmag * jnp.cos(j * (dt * a_im)[:, None, :])
    pw_im = pmag * jnp.sin(j * (dt * a_im)[:, None, :])
    zb_re = z_re[:, :, None] * b_re - z_im[:, :, None] * b_im
    zb_im = z_re[:, :, None] * b_im + z_im[:, :, None] * b_re
    e_re = pw_re[..., None] * zb_re[:, None] - pw_im[..., None] * zb_im[:, None]
    e_im = pw_re[..., None] * zb_im[:, None] + pw_im[..., None] * zb_re[:, None]
    hp = lax.Precision.HIGHEST
    kc = (jnp.einsum('gip,gjpn->gjin', c_re, e_re[:, :t], precision=hp)
          - jnp.einsum('gip,gjpn->gjin', c_im, e_im[:, :t], precision=hp))
    lag = jnp.arange(t)[None, :] - jnp.arange(t)[:, None]
    kt = jnp.where((lag >= 0)[None, :, :, None, None], kc[:, jnp.clip(lag, 0, t - 1)], 0.0)
    kt = kt.transpose(0, 1, 4, 2, 3).reshape(g, t * n_in, t * n_in)
    w_re = e_re[:, :t][:, ::-1].transpose(0, 1, 3, 2).reshape(g, t * n_in, p)
    w_im = e_im[:, :t][:, ::-1].transpose(0, 1, 3, 2).reshape(g, t * n_in, p)
    w = jnp.concatenate([w_re, w_im], axis=-1)
    q_re, q_im = pw_re[:, 1:], pw_im[:, 1:]
    v_re = c_re[:, None] * q_re[:, :, None, :] - c_im[:, None] * q_im[:, :, None, :]
    v_im = c_re[:, None] * q_im[:, :, None, :] + c_im[:, None] * q_re[:, :, None, :]
    v = jnp.concatenate([v_re.transpose(0, 3, 1, 2).reshape(g, p, t * n_in),
                         (-v_im).transpose(0, 3, 1, 2).reshape(g, p, t * n_in)], axis=1)
    dec = jnp.stack([jnp.concatenate([pw_re[:, t], pw_re[:, t]], -1),
                     jnp.concatenate([-pw_im[:, t], pw_im[:, t]], -1)], axis=1)
    return kt.astype(BF16), w.astype(BF16), v.astype(BF16), dec


def _rope_tables(positions):
    pos = positions.astype(F32).reshape(-1, 1)

    def cs(half):
        inv = jnp.power(ROPE_THETA, -jnp.arange(half, dtype=F32) / half)
        ang = pos * inv
        return jnp.cos(ang), jnp.sin(ang)

    c, s = cs(64)
    c128, s128 = jnp.concatenate([c, c], -1), jnp.concatenate([-s, s], -1)
    c, s = cs(32)
    c64, s64 = jnp.concatenate([c, c, c, c], -1), jnp.concatenate([-s, s, -s, s], -1)
    return c128, s128, c64, s64


def _layer_weights(l, prm):
    lw = {}
    lw['ff1'] = _prep_ffn_weights(prm['w_ff1_in'][l], prm['w_ff1_out'][l])
    lw['ff2'] = _prep_ffn_weights(prm['w_ff2_in'][l], prm['w_ff2_out'][l])
    lw['w_in'] = _prep_w_in(prm['w_in'][l])
    lw['dsa_norm'] = prm['dsa_qk_norm'][l]
    lw['mla_q_norm'] = prm['mla_q_norm'][l].reshape(1, -1)
    lw['mla_kv_norm'] = prm['mla_kv_norm'][l].reshape(1, -1)
    nd, rd, vd = MLA_NOPE_DIM, MLA_ROPE_DIM, MLA_V_DIM
    wuq = prm['mla_w_uq'][l].reshape(MLA_Q_RANK, MLA_HEADS, nd + rd)
    lw['w_uq'] = jnp.pad(wuq, ((0, 0), (0, 0), (0, 2 * LANES - nd - rd))).reshape(MLA_Q_RANK, -1).astype(BF16)
    wukv = prm['mla_w_ukv'][l].reshape(MLA_KV_RANK, MLA_HEADS, nd + vd)
    lw['w_ukn'] = wukv[:, :, :nd].reshape(MLA_KV_RANK, -1).astype(BF16)
    lw['w_uv'] = wukv[:, :, nd:].reshape(MLA_KV_RANK, -1).astype(BF16)
    qkn = prm['mla_qk_norm'][l]
    zpad = jnp.zeros((LANES - rd,), F32)
    lw['mla_qk_norm'] = jnp.stack([qkn[0, :nd], jnp.concatenate([qkn[0, nd:], zpad]),
                                   qkn[1, :nd], jnp.concatenate([zpad, qkn[1, nd:]])])
    w2 = jnp.zeros((LANES, GLA_HEADS * GLA_KEY_DIM), F32)
    lw['gla_w2'] = w2.at[MISC_GLR:MISC_GLR + GLA_GATE_RANK].set(prm['gla_w_gate'][l])
    lw['gla_b2'] = prm['gla_b_gate'][l].reshape(1, -1)
    lw['gla_on'] = prm['gla_out_norm'][l].reshape(1, -1)
    ind = np.kron(np.eye(GLA_HEADS, dtype=np.float32), np.ones((GLA_KEY_DIM, GLA_VAL_DIM), np.float32))
    lw['gla_ind'] = jnp.asarray(ind, BF16)
    lw['s5_kt'], lw['s5_w'], lw['s5_v'], lw['s5_dec'] = _prep_s5(
        prm['s5_a_re'][l], prm['s5_a_im'][l], prm['s5_log_dt'][l], prm['s5_b_re'][l], prm['s5_b_im'][l],
        prm['s5_c_re'][l], prm['s5_c_im'][l])
    lw['s5_d'] = prm['s5_d'][l].reshape(1, -1)
    lw['s5_w_glu'] = prm['s5_w_glu'][l].astype(BF16)
    lw['s5_b_glu'] = prm['s5_b_glu'][l].reshape(1, -1)
    lw['w_gate'] = prm['w_gate'][l].astype(BF16)
    lw['w_branch'] = prm['w_branch'][l].astype(BF16)
    lw['w_out'] = prm['w_out'][l].astype(BF16)
    return lw


def _mixers(p2, tabs, lw, bsz, seq, k_sel):
    qa, ka, va, iq, ik2, qm, km, vm = _prep(p2, tabs, lw)
    br_a = _dsa(qa, iq, p2, ka, va, ik2, bsz, seq, k_sel)
    br_b = _gla(p2, lw, bsz, seq)
    br_c = _s5(p2, lw, bsz, seq)
    br_d = _mla(qm, km, vm, bsz, seq)
    return br_a, br_b, br_c, br_d


def kernel(x, c, positions, w_ada, b_ada, norm_g, w_ff1_in, w_ff1_out, w_ff2_in, w_ff2_out, w_in, dsa_qk_norm, gla_w_gate, gla_b_gate, gla_out_norm, s5_a_re, s5_a_im, s5_log_dt, s5_b_re, s5_b_im, s5_c_re, s5_c_im, s5_d, s5_w_glu, s5_b_glu, mla_q_norm, mla_kv_norm, mla_w_uq, mla_w_ukv, mla_qk_norm, w_branch, w_gate, w_out):
    prm = dict(w_ff1_in=w_ff1_in, w_ff1_out=w_ff1_out, w_ff2_in=w_ff2_in, w_ff2_out=w_ff2_out, w_in=w_in,
               dsa_qk_norm=dsa_qk_norm, gla_w_gate=gla_w_gate, gla_b_gate=gla_b_gate, gla_out_norm=gla_out_norm,
               s5_a_re=s5_a_re, s5_a_im=s5_a_im, s5_log_dt=s5_log_dt, s5_b_re=s5_b_re, s5_b_im=s5_b_im,
               s5_c_re=s5_c_re, s5_c_im=s5_c_im, s5_d=s5_d, s5_w_glu=s5_w_glu, s5_b_glu=s5_b_glu,
               mla_q_norm=mla_q_norm, mla_kv_norm=mla_kv_norm, mla_w_uq=mla_w_uq, mla_w_ukv=mla_w_ukv,
               mla_qk_norm=mla_qk_norm, w_branch=w_branch, w_gate=w_gate, w_out=w_out)
    bsz, seq, d = x.shape
    depth = w_ada.shape[0]
    k_sel = min(INDEX_TOPK, seq // 4)
    mod = _ada_all(c, w_ada, b_ada).reshape(depth, bsz, N_MOD, d)
    tabs = _rope_tables(positions)
    x2 = x.reshape(bsz * seq, d)
    for l in range(depth):
        lw = _layer_weights(l, prm)
        x2 = _ffn(x2, mod[l], norm_g[l, 0], *lw['ff1'], seq, 0)
        h2, p2 = _inproj(x2, mod[l], norm_g[l, 1], lw['w_in'], seq)
        branches = _mixers(p2, tabs, lw, bsz, seq, k_sel)
        merged = _merge(h2, branches, lw['w_gate'], lw['w_branch'])
        x2 = _outproj(x2, mod[l], merged, lw['w_out'], seq)
        x2 = _ffn(x2, mod[l], norm_g[l, 2], *lw['ff2'], seq, 6)
    return x2.reshape(bsz, seq, d)
```

```python
import functools
import math

import numpy as np
import jax
import jax.numpy as jnp
from jax import lax
from jax.experimental import pallas as pl
from jax.experimental.pallas import tpu as pltpu

F32 = jnp.float32
BF16 = jnp.bfloat16
I32 = jnp.int32

N_BRANCHES = 4
DSA_HEADS = 4
DSA_HEAD_DIM = 128
IDX_HEADS = 8
IDX_DIM = 64
INDEX_TOPK = 256
GLA_HEADS = 4
GLA_KEY_DIM = 64
GLA_VAL_DIM = 128
GLA_GATE_RANK = 16
GLA_TAU = 16.0
GLA_CHUNK = 64
GLA_SUB = 16
GLA_CHUNKS_PER_STEP = 4
S5_GROUP = 16
S5_STATE = 64
S5_CHUNK = 8
MLA_HEADS = 4
MLA_NOPE_DIM = 128
MLA_ROPE_DIM = 64
MLA_V_DIM = 128
MLA_Q_RANK = 384
MLA_KV_RANK = 128
ROPE_THETA = 10000.0
NORM_EPS = 1e-6
N_MOD = 9

LANES = 128
FF_TILE = 512
ROW_TILE = 512
ATT_TILE = 256
KEY_TILES = 2
ATT_ROWS = 64
LOG2E = math.log2(math.e)
SEARCH_ROWS = 128
INT_MIN = -(2 ** 31)
NEG_BIG = -1e30

P_AQ, P_AK, P_AV, P_IQ = 0, 512, 1024, 1536
P_GQ, P_GK, P_GV, P_GR = 2048, 2304, 2560, 3072
P_SU, P_MQKV, P_IKR, P_MISC = 3584, 4096, 4608, 4736
P_COLS = 4864
MISC_IW, MISC_GLR = 0, 8


def _cparams(sem, vmem_mb):
    return pltpu.CompilerParams(dimension_semantics=sem, vmem_limit_bytes=vmem_mb << 20)


def _dot(a, b, precision=None):
    return jnp.dot(a, b, preferred_element_type=F32, precision=precision)


def _dot_nt(a, b):
    return lax.dot_general(a, b, (((1,), (1,)), ((), ())), preferred_element_type=F32)


def _dot_tn(a, b):
    return lax.dot_general(a, b, (((0,), (0,)), ((), ())), preferred_element_type=F32)


def _rms(x, dim):
    return x * lax.rsqrt(jnp.sum(x * x, axis=-1, keepdims=True) * (1.0 / dim) + NORM_EPS)


def _lane_tile(x, n):
    return x if n == 1 else jnp.concatenate([x] * n, axis=1)


def _ada_kernel(c_ref, w_ref, b_ref, o_ref):
    c = c_ref[...]
    cond = (c * jax.nn.sigmoid(c)).astype(BF16)
    o_ref[0] = _dot(cond, w_ref[0].astype(BF16)) + b_ref[0]


def _ada_all(c, w_ada, b_ada):
    depth, d, n = w_ada.shape
    bsz = c.shape[0]
    tn = 1024
    assert n % tn == 0
    return pl.pallas_call(
        _ada_kernel,
        grid=(depth, n // tn),
        in_specs=[
            pl.BlockSpec((bsz, d), lambda l, j: (0, 0)),
            pl.BlockSpec((1, d, tn), lambda l, j: (l, 0, j)),
            pl.BlockSpec((1, 1, tn), lambda l, j: (l, 0, j)),
        ],
        out_specs=pl.BlockSpec((1, bsz, tn), lambda l, j: (l, 0, j)),
        out_shape=jax.ShapeDtypeStruct((depth, bsz, n), F32),
        compiler_params=_cparams(("parallel", "parallel"), 48),
        name="ada_mod",
    )(c, w_ada, b_ada.reshape(depth, 1, n))


def _ada_norm_tile(x, gain, shift, scale):
    d = x.shape[-1]
    return _rms(x, d) * gain * (1.0 + scale) + shift


def _ffn_kernel(x_ref, mod_ref, g_ref, wa_ref, wb_ref, wo_ref, o_ref, h_scr, acc_scr, *, k0):
    f = pl.program_id(1)

    @pl.when(f == 0)
    def _():
        h = _ada_norm_tile(x_ref[...], g_ref[...], mod_ref[0, k0:k0 + 1, :], mod_ref[0, k0 + 1:k0 + 2, :])
        h_scr[...] = h.astype(BF16)
        acc_scr[...] = jnp.zeros_like(acc_scr)

    h = h_scr[...]
    a = _dot(h, wa_ref[...])
    b = _dot(h, wb_ref[...])
    u = (a * jax.nn.sigmoid(a)) * b
    acc_scr[...] += _dot(u.astype(BF16), wo_ref[...])

    @pl.when(f == pl.num_programs(1) - 1)
    def _():
        o_ref[...] = x_ref[...] + (0.5 * mod_ref[0, k0 + 2:k0 + 3, :]) * acc_scr[...]


def _ffn(x2, mod_l, gain, w_in_p, w_out_p, seq, k0):
    m, d = x2.shape
    ffp = w_out_p.shape[0]
    tm, tf = ROW_TILE, FF_TILE
    nf = ffp // tf
    tiles_per_batch = seq // tm
    return pl.pallas_call(
        functools.partial(_ffn_kernel, k0=k0),
        grid=(m // tm, nf),
        in_specs=[
            pl.BlockSpec((tm, d), lambda i, f: (i, 0)),
            pl.BlockSpec((1, N_MOD, d), lambda i, f: (i // tiles_per_batch, 0, 0)),
            pl.BlockSpec((1, d), lambda i, f: (0, 0)),
            pl.BlockSpec((d, tf), lambda i, f: (0, f)),
            pl.BlockSpec((d, tf), lambda i, f: (0, f + nf)),
            pl.BlockSpec((tf, d), lambda i, f: (f, 0)),
        ],
        out_specs=pl.BlockSpec((tm, d), lambda i, f: (i, 0)),
        out_shape=jax.ShapeDtypeStruct((m, d), F32),
        scratch_shapes=[pltpu.VMEM((tm, d), BF16), pltpu.VMEM((tm, d), F32)],
        compiler_params=_cparams(("parallel", "arbitrary"), 56),
        name="ffn",
    )(x2, mod_l, gain.reshape(1, d), w_in_p, w_in_p, w_out_p)


def _inproj_kernel(x_ref, mod_ref, g_ref, w_ref, h_ref, p_ref):
    @pl.when(pl.program_id(1) == 0)
    def _():
        h = _ada_norm_tile(x_ref[...], g_ref[...], mod_ref[0, 3:4, :], mod_ref[0, 4:5, :])
        h_ref[...] = h.astype(BF16)

    p_ref[...] = _dot(h_ref[...], w_ref[...])


def _inproj(x2, mod_l, gain, w_in_p, seq):
    m, d = x2.shape
    n = w_in_p.shape[1]
    tm = ROW_TILE
    tn = n // 2
    tiles_per_batch = seq // tm
    return pl.pallas_call(
        _inproj_kernel,
        grid=(m // tm, n // tn),
        in_specs=[
            pl.BlockSpec((tm, d), lambda i, j: (i, 0)),
            pl.BlockSpec((1, N_MOD, d), lambda i, j: (i // tiles_per_batch, 0, 0)),
            pl.BlockSpec((1, d), lambda i, j: (0, 0)),
            pl.BlockSpec((d, tn), lambda i, j: (0, j)),
        ],
        out_specs=[
            pl.BlockSpec((tm, d), lambda i, j: (i, 0)),
            pl.BlockSpec((tm, tn), lambda i, j: (i, j)),
        ],
        out_shape=[jax.ShapeDtypeStruct((m, d), BF16), jax.ShapeDtypeStruct((m, n), F32)],
        compiler_params=_cparams(("parallel", "arbitrary"), 56),
        name="inproj",
    )(x2, mod_l, gain.reshape(1, d), w_in_p)


def _rope128(x, cos, sin):
    return x * cos + pltpu.roll(x, 64, 1) * sin


def _rope64(x, cos, sin, lane):
    rot = jnp.where((lane & 63) < 32, pltpu.roll(x, 96, 1), pltpu.roll(x, 32, 1))
    return x * cos + rot * sin


def _prep_kernel(aq_ref, ak_ref, av_ref, iq_ref, mqkv_ref, ikr_ref,
                 c128_ref, s128_ref, c64_ref, s64_ref,
                 dsan_ref, mqn_ref, mkvn_ref, wuq_ref, wukn_ref, wuv_ref, mqkn_ref,
                 qa_o, ka_o, va_o, iq_o, ik2_o, qm_o, km_o, vm_o):
    tm = aq_ref.shape[0]
    lane = lax.broadcasted_iota(I32, (tm, LANES), 1)
    low = lane < 64
    c128, s128 = c128_ref[...], s128_ref[...]
    c64, s64 = c64_ref[...], s64_ref[...]

    aq, ak = aq_ref[...], ak_ref[...]
    for h in range(DSA_HEADS):
        sl = slice(h * LANES, (h + 1) * LANES)
        qh = _rope128(_rms(aq[:, sl], DSA_HEAD_DIM) * dsan_ref[0:1, :], c128, s128)
        kh = _rope128(_rms(ak[:, sl], DSA_HEAD_DIM) * dsan_ref[1:2, :], c128, s128)
        qa_o[:, sl] = (qh * (DSA_HEAD_DIM ** -0.5 * LOG2E)).astype(BF16)
        ka_o[:, sl] = kh.astype(BF16)
    va_o[...] = av_ref[...].astype(BF16)

    iq = iq_ref[...]
    for blk in range(IDX_HEADS * IDX_DIM // LANES):
        sl = slice(blk * LANES, (blk + 1) * LANES)
        iq_o[:, sl] = _rope64(iq[:, sl], c64, s64, lane).astype(BF16)

    x = ikr_ref[...]
    ms = jnp.sum(jnp.where(low, 0.0, x * x), axis=-1, keepdims=True) * (1.0 / MLA_ROPE_DIM)
    xn = jnp.where(low, x, x * lax.rsqrt(ms + NORM_EPS) * mqkn_ref[3:4, :])
    xr = _rope64(xn, c64, s64, lane)
    xs = pltpu.roll(xr, 64, 1)
    ik2_o[:, 0:LANES] = jnp.where(low, xr, 0.0).astype(BF16)
    ik2_o[:, LANES:2 * LANES] = jnp.where(low, 0.0, xs).astype(BF16)
    kr128 = jnp.where(low, xs, 0.0).astype(BF16)

    mqkv = mqkv_ref[...]
    ql = (_rms(mqkv[:, 0:MLA_Q_RANK], MLA_Q_RANK) * mqn_ref[...]).astype(BF16)
    kvl = (_rms(mqkv[:, MLA_Q_RANK:MLA_Q_RANK + MLA_KV_RANK], MLA_KV_RANK) * mkvn_ref[...]).astype(BF16)
    qf = _dot(ql, wuq_ref[...])
    kn = _dot(kvl, wukn_ref[...])
    vm_o[...] = _dot(kvl, wuv_ref[...]).astype(BF16)
    qscale = (MLA_NOPE_DIM + MLA_ROPE_DIM) ** -0.5 * LOG2E
    for h in range(MLA_HEADS):
        qn = _rms(qf[:, h * 256:h * 256 + LANES], MLA_NOPE_DIM) * mqkn_ref[0:1, :]
        qr_raw = qf[:, h * 256 + LANES:(h + 1) * 256]
        msr = jnp.sum(qr_raw * qr_raw, axis=-1, keepdims=True) * (1.0 / MLA_ROPE_DIM)
        qr = qr_raw * lax.rsqrt(msr + NORM_EPS) * mqkn_ref[1:2, :]
        qr = jnp.where(low, _rope64(qr, c64, s64, lane), 0.0)
        qm_o[:, h * 256:h * 256 + LANES] = (qn * qscale).astype(BF16)
        qm_o[:, h * 256 + LANES:(h + 1) * 256] = (qr * qscale).astype(BF16)
        kh = _rms(kn[:, h * LANES:(h + 1) * LANES], MLA_NOPE_DIM) * mqkn_ref[2:3, :]
        km_o[:, h * 256:h * 256 + LANES] = kh.astype(BF16)
        km_o[:, h * 256 + LANES:(h + 1) * 256] = kr128


def _prep(p2, tabs, lw):
    m = p2.shape[0]
    tm = ROW_TILE

    def col(width, off):
        assert off % width == 0
        return pl.BlockSpec((tm, width), lambda i: (i, off // width))

    def full(a):
        return pl.BlockSpec(a.shape, lambda i: (0,) * a.ndim)

    tab = pl.BlockSpec((tm, LANES), lambda i: (i, 0))
    consts = [lw['dsa_norm'], lw['mla_q_norm'], lw['mla_kv_norm'], lw['w_uq'], lw['w_ukn'], lw['w_uv'],
              lw['mla_qk_norm']]
    outs = [(512, BF16)] * 4 + [(256, BF16), (1024, BF16), (1024, BF16), (512, BF16)]
    return pl.pallas_call(
        _prep_kernel,
        grid=(m // tm,),
        in_specs=[col(512, P_AQ), col(512, P_AK), col(512, P_AV), col(512, P_IQ), col(512, P_MQKV),
                  col(LANES, P_IKR), tab, tab, tab, tab] + [full(a) for a in consts],
        out_specs=[pl.BlockSpec((tm, w), lambda i: (i, 0)) for w, _ in outs],
        out_shape=[jax.ShapeDtypeStruct((m, w), dt) for w, dt in outs],
        compiler_params=_cparams(("parallel",), 48),
        name="mixer_prep",
    )(p2, p2, p2, p2, p2, p2, *tabs, *consts)


def _flash_update(h, s_scr, p_scr, alpha_scr, m_scr, acc_scr, v_ext):
    _, tq, tk = s_scr.shape
    nl = tk // LANES
    for r0 in range(0, tq, ATT_ROWS):
        rows = slice(r0, r0 + ATT_ROWS)
        s = s_scr[h, rows, :]
        m_prev = m_scr[h, rows, :]
        mx = s[:, 0:LANES]
        for t in range(1, nl):
            mx = jnp.maximum(mx, s[:, t * LANES:(t + 1) * LANES])
        m_new = jnp.maximum(m_prev, jnp.max(mx, axis=1, keepdims=True))
        alpha_scr[h, rows, :] = jnp.exp2(m_prev - m_new)
        m_scr[h, rows, :] = m_new
        p_scr[h, rows, :] = jnp.exp2(s - _lane_tile(m_new, nl)).astype(BF16)
    na = acc_scr.shape[2] // LANES
    acc_scr[h] = acc_scr[h] * _lane_tile(alpha_scr[h], na) + _dot(p_scr[h], v_ext)


def _flash_init(m_scr, acc_scr):
    m_scr[...] = jnp.full(m_scr.shape, NEG_BIG, F32)
    acc_scr[...] = jnp.zeros(acc_scr.shape, F32)


def _flash_finish(o_ref, acc_scr, heads, dv):
    for h in range(heads):
        a = acc_scr[h]
        o_ref[0, :, h * dv:(h + 1) * dv] = (a[:, 0:dv] / a[:, dv:dv + LANES]).astype(BF16)


def _flash_scratch(heads, tq, dv):
    tk = KEY_TILES * tq
    return [pltpu.VMEM((heads, tq, tk), F32), pltpu.VMEM((heads, tq, tk), BF16),
            pltpu.VMEM((heads, tq, LANES), F32), pltpu.VMEM((heads, tq, LANES), F32),
            pltpu.VMEM((heads, tq, dv + LANES), F32)]


def _dsa_kernel(q_ref, iq_ref, misc_ref, k_ref, v_ref, ik2_ref, o_ref,
                key_scr, w_scr, thr_scr, bias_scr, s_scr, p_scr, alpha_scr, m_scr, acc_scr,
                *, k_sel, idx_bits):
    tq = q_ref.shape[1]
    nl = tq // LANES
    rb = SEARCH_ROWS
    qi = pl.program_id(1)
    nt = qi + 1
    kf = float(k_sel)

    w = misc_ref[...] * (IDX_HEADS ** -0.5 * IDX_DIM ** -0.5)
    for h in range(IDX_HEADS):
        w_scr[h] = jnp.broadcast_to(w[:, MISC_IW + h:MISC_IW + h + 1], (tq, LANES))
    row = lax.broadcasted_iota(I32, (tq, tq), 0)
    colv = lax.broadcasted_iota(I32, (tq, tq), 1)

    def score_tile(j, carry):
        ks = pl.multiple_of(j * tq, tq)
        ikl = ik2_ref[0, pl.ds(ks, tq), 0:LANES]
        ikh = ik2_ref[0, pl.ds(ks, tq), LANES:2 * LANES]
        acc = jnp.zeros((tq, tq), F32)
        for hp in range(IDX_HEADS // 2):
            a = iq_ref[0, :, hp * LANES:(hp + 1) * LANES]
            acc = acc + _lane_tile(w_scr[2 * hp], nl) * jnp.maximum(_dot_nt(a, ikl), 0.0)
            acc = acc + _lane_tile(w_scr[2 * hp + 1], nl) * jnp.maximum(_dot_nt(a, ikh), 0.0)
        acc = jnp.where(acc == 0.0, 0.0, acc)
        bits = pltpu.bitcast(acc, I32)
        key = bits ^ ((bits >> 31) & 0x7FFFFFFF)
        key_scr[j] = jnp.where(colv + (j - qi) * tq <= row, key, INT_MIN)
        return carry

    lax.fori_loop(0, nt, score_tile, 0)

    blocks = tuple(range(0, tq, rb))

    def lane_partial(r0, pred):
        def body(j, acc):
            hit = jnp.where(pred(key_scr[j, r0:r0 + rb, :], j), 1.0, 0.0)
            for t in range(nl):
                acc = acc + hit[:, t * LANES:(t + 1) * LANES]
            return acc
        return lax.fori_loop(0, nt, body, jnp.zeros((rb, LANES), F32))

    def lane_total(acc):
        return jnp.broadcast_to(jnp.sum(acc, axis=1, keepdims=True), (rb, LANES))

    def count(r0, pred):
        return lane_total(lane_partial(r0, pred))

    def count_ge(values):
        parts = [lane_partial(r0, lambda key, j, vw=_lane_tile(v, nl): key >= vw)
                 for r0, v in zip(blocks, values)]
        return [lane_total(p) for p in parts]

    zeros = [jnp.zeros((rb, LANES), I32) for _ in blocks]
    lo0 = tuple(jnp.where(c >= kf, 0, INT_MIN).astype(I32) for c in count_ge(zeros))

    def bs_body(i, los):
        cands = [lo | jnp.left_shift(jnp.int32(1), 30 - i) for lo in los]
        return tuple(jnp.where(c >= kf, cand, lo) for c, cand, lo in zip(count_ge(cands), cands, los))

    thrs = lax.fori_loop(0, 31, bs_body, lo0)
    tie_flags = []
    for r0, thr, c_ge in zip(blocks, thrs, count_ge(thrs)):
        thr_scr[r0:r0 + rb, :] = thr
        tie_flags.append(jnp.max(jnp.where((c_ge > kf) & (thr > INT_MIN), 1.0, 0.0)))

    @pl.when(functools.reduce(jnp.maximum, tie_flags) > 0.0)
    def _():
        cols = lax.broadcasted_iota(I32, (rb, tq), 1)
        for r0 in range(0, tq, rb):
            thr = thr_scr[r0:r0 + rb, :]
            thr_w = _lane_tile(thr, nl)
            need = kf - count(r0, lambda key, j: key > thr_w)

            def ib_body(i, p, r0=r0, thr_w=thr_w, need=need):
                cand = p | jnp.left_shift(jnp.int32(1), idx_bits - 1 - i)
                cand_w = _lane_tile(cand, nl)
                c = count(r0, lambda key, j: (key == thr_w) & (cols + j * tq < cand_w))
                return jnp.where(c < need, cand, p)

            last_w = _lane_tile(lax.fori_loop(0, idx_bits, ib_body, jnp.zeros((rb, LANES), I32)), nl)

            def demote(j, carry, r0=r0, thr_w=thr_w, last_w=last_w):
                key = key_scr[j, r0:r0 + rb, :]
                key_scr[j, r0:r0 + rb, :] = jnp.where((key == thr_w) & (cols + j * tq > last_w), INT_MIN, key)
                return carry

            lax.fori_loop(0, nt, demote, 0)

    _flash_init(m_scr, acc_scr)
    tk = KEY_TILES * tq
    ones = jnp.ones((tk, LANES), BF16)

    @pl.when(nt % KEY_TILES != 0)
    def _():
        key_scr[nt] = jnp.full((tq, tq), INT_MIN, I32)

    def att_tiles(jj, carry):
        ks = pl.multiple_of(jj * tk, tk)
        thr_w = _lane_tile(jnp.maximum(thr_scr[...], INT_MIN + 1), nl)
        for t in range(KEY_TILES):
            bias_scr[:, t * tq:(t + 1) * tq] = jnp.where(key_scr[KEY_TILES * jj + t] >= thr_w, 0.0, NEG_BIG)
        for h in range(DSA_HEADS):
            sl = slice(h * LANES, (h + 1) * LANES)
            s_scr[h] = _dot_nt(q_ref[0, :, sl], k_ref[0, pl.ds(ks, tk), sl]) + bias_scr[...]
            v_ext = jnp.concatenate([v_ref[0, pl.ds(ks, tk), sl], ones], axis=1)
            _flash_update(h, s_scr, p_scr, alpha_scr, m_scr, acc_scr, v_ext)
        return carry

    lax.fori_loop(0, (nt + KEY_TILES - 1) // KEY_TILES, att_tiles, 0)
    _flash_finish(o_ref, acc_scr, DSA_HEADS, DSA_HEAD_DIM)


def _dsa(qa, iq, p2, ka, va, ik2, bsz, seq, k_sel):
    tq = ATT_TILE
    assert k_sel <= tq and seq % (KEY_TILES * tq) == 0 and tq % SEARCH_ROWS == 0
    nq = seq // tq
    hw = DSA_HEADS * DSA_HEAD_DIM
    r3 = lambda a: a.reshape(bsz, seq, a.shape[-1])
    return pl.pallas_call(
        functools.partial(_dsa_kernel, k_sel=k_sel, idx_bits=int(math.ceil(math.log2(seq)))),
        grid=(bsz, nq),
        in_specs=[
            pl.BlockSpec((1, tq, hw), lambda b, i: (b, i, 0)),
            pl.BlockSpec((1, tq, IDX_HEADS * IDX_DIM), lambda b, i: (b, i, 0)),
            pl.BlockSpec((tq, LANES), lambda b, i: (b * nq + i, P_MISC // LANES)),
            pl.BlockSpec((1, seq, hw), lambda b, i: (b, 0, 0)),
            pl.BlockSpec((1, seq, hw), lambda b, i: (b, 0, 0)),
            pl.BlockSpec((1, seq, 2 * LANES), lambda b, i: (b, 0, 0)),
        ],
        out_specs=pl.BlockSpec((1, tq, hw), lambda b, i: (b, i, 0)),
        out_shape=jax.ShapeDtypeStruct((bsz, seq, hw), BF16),
        scratch_shapes=[pltpu.VMEM((nq, tq, tq), I32), pltpu.VMEM((IDX_HEADS, tq, LANES), F32),
                        pltpu.VMEM((tq, LANES), I32), pltpu.VMEM((tq, KEY_TILES * tq), F32)]
        + _flash_scratch(DSA_HEADS, tq, DSA_HEAD_DIM),
        compiler_params=_cparams(("parallel", "arbitrary"), 48),
        name="dsa",
    )(r3(qa), r3(iq), p2, r3(ka), r3(va), r3(ik2)).reshape(bsz * seq, hw)


def _mla_kernel(q_ref, k_ref, v_ref, o_ref, s_scr, p_scr, alpha_scr, m_scr, acc_scr):
    tq = q_ref.shape[1]
    qi = pl.program_id(1)
    dq = q_ref.shape[2] // MLA_HEADS
    _flash_init(m_scr, acc_scr)
    tk = KEY_TILES * tq
    ones = jnp.ones((tk, LANES), BF16)

    def tiles(jj, carry, last):
        ks = pl.multiple_of(jj * tk, tk)
        for h in range(MLA_HEADS):
            s = _dot_nt(q_ref[0, :, h * dq:(h + 1) * dq], k_ref[0, pl.ds(ks, tk), h * dq:(h + 1) * dq])
            if last:
                kpos = ks + lax.broadcasted_iota(I32, (tq, tk), 1)
                qpos = qi * tq + lax.broadcasted_iota(I32, (tq, tk), 0)
                s = jnp.where(kpos <= qpos, s, NEG_BIG)
            s_scr[h] = s
            v_ext = jnp.concatenate([v_ref[0, pl.ds(ks, tk), h * MLA_V_DIM:(h + 1) * MLA_V_DIM], ones], axis=1)
            _flash_update(h, s_scr, p_scr, alpha_scr, m_scr, acc_scr, v_ext)
        return carry

    n_full = qi // KEY_TILES
    lax.fori_loop(0, n_full, functools.partial(tiles, last=False), 0)
    tiles(n_full, 0, True)
    _flash_finish(o_ref, acc_scr, MLA_HEADS, MLA_V_DIM)


def _mla(qm, km, vm, bsz, seq):
    tq = ATT_TILE
    assert seq % (KEY_TILES * tq) == 0
    nq = seq // tq
    r3 = lambda a: a.reshape(bsz, seq, a.shape[-1])
    qw, vw = qm.shape[-1], vm.shape[-1]
    return pl.pallas_call(
        _mla_kernel,
        grid=(bsz, nq),
        in_specs=[
            pl.BlockSpec((1, tq, qw), lambda b, i: (b, i, 0)),
            pl.BlockSpec((1, seq, qw), lambda b, i: (b, 0, 0)),
            pl.BlockSpec((1, seq, vw), lambda b, i: (b, 0, 0)),
        ],
        out_specs=pl.BlockSpec((1, tq, vw), lambda b, i: (b, i, 0)),
        out_shape=jax.ShapeDtypeStruct((bsz, seq, vw), BF16),
        scratch_shapes=_flash_scratch(MLA_HEADS, tq, MLA_V_DIM),
        compiler_params=_cparams(("parallel", "arbitrary"), 48),
        name="mla",
    )(r3(qm), r3(km), r3(vm)).reshape(bsz * seq, vw)


def _gla_chunk(r0, gq_ref, gk_ref, gv_ref, gr_ref, misc_ref, w2_ref, b2_ref, on_ref, ind_ref, o_ref, st_scr):
    ch, sub = GLA_CHUNK, GLA_SUB
    dk, dv = GLA_KEY_DIM, GLA_VAL_DIM
    hi = lax.Precision.HIGHEST
    rows = slice(r0, r0 + ch)

    q = gq_ref[rows, :] * dk ** -0.5
    k = gk_ref[rows, :]
    v = gv_ref[rows, :]
    vb = v.astype(BF16)
    x = _dot(misc_ref[rows, :], w2_ref[...], hi) + b2_ref[...]
    log_a = (jnp.minimum(x, 0.0) - jnp.log1p(jnp.exp(-jnp.abs(x)))) * (1.0 / GLA_TAU)
    tri = (lax.broadcasted_iota(I32, (ch, ch), 0) >= lax.broadcasted_iota(I32, (ch, ch), 1)).astype(F32)
    b = _dot(tri, log_a, hi)

    tcol = lax.broadcasted_iota(I32, (sub, 1), 0)
    blocks = []
    for i in range(ch // sub):
        s0 = i * sub
        qs, ks, bs, vs = q[s0:s0 + sub], k[s0:s0 + sub], b[s0:s0 + sub], v[s0:s0 + sub]
        ps = []
        for j in range(sub):
            e = jnp.exp(jnp.minimum(bs - bs[j:j + 1], 0.0))
            ps.append(jnp.where(tcol >= j, qs * (ks[j:j + 1] * e), 0.0))
        r = _dot(jnp.concatenate(ps, axis=0).astype(BF16), ind_ref[...])
        od = r[0:sub] * vs[0:1]
        for j in range(1, sub):
            od = od + r[j * sub:(j + 1) * sub] * vs[j:j + 1]
        if i > 0:
            ref_b = b[s0 - 1:s0]
            qd = (qs * jnp.exp(bs - ref_b)).astype(BF16)
            kd = (k[0:s0] * jnp.exp(ref_b - b[0:s0])).astype(BF16)
            parts = []
            for h in range(GLA_HEADS):
                a = _dot_nt(qd[:, h * dk:(h + 1) * dk], kd[:, h * dk:(h + 1) * dk])
                parts.append(_dot(a.astype(BF16), vb[0:s0, h * dv:(h + 1) * dv]))
            od = od + jnp.concatenate(parts, axis=1)
        blocks.append(od)
    o = jnp.concatenate(blocks, axis=0)

    qe = (q * jnp.exp(b)).astype(BF16)
    b_last = b[ch - 1:ch]
    kdec = (k * jnp.exp(b_last - b)).astype(BF16)
    gr = gr_ref[rows, :]
    for h in range(GLA_HEADS):
        st = st_scr[h]
        oh = o[:, h * dv:(h + 1) * dv] + _dot_nt(qe[:, h * dk:(h + 1) * dk], st.astype(BF16))
        oh = _rms(oh, dv) * on_ref[...]
        g = gr[:, h * dv:(h + 1) * dv]
        o_ref[rows, h * dv:(h + 1) * dv] = (oh * (g * jax.nn.sigmoid(g))).astype(BF16)
        st_scr[h] = st * jnp.exp(b_last[:, h * dk:(h + 1) * dk]) + _dot_tn(
            vb[:, h * dv:(h + 1) * dv], kdec[:, h * dk:(h + 1) * dk])


def _gla_kernel(*refs, chunks):
    st_scr = refs[-1]

    @pl.when(pl.program_id(1) == 0)
    def _():
        st_scr[...] = jnp.zeros_like(st_scr)

    for cc in range(chunks):
        _gla_chunk(cc * GLA_CHUNK, *refs)


def _gla(p2, lw, bsz, seq):
    cps = math.gcd(GLA_CHUNKS_PER_STEP, seq // GLA_CHUNK)
    rows = cps * GLA_CHUNK
    ns = seq // rows
    hk, hv = GLA_HEADS * GLA_KEY_DIM, GLA_HEADS * GLA_VAL_DIM

    def col(width, off):
        assert off % width == 0
        return pl.BlockSpec((rows, width), lambda b, c: (b * ns + c, off // width))

    def full(a):
        return pl.BlockSpec(a.shape, lambda b, c: (0,) * a.ndim)

    consts = [lw['gla_w2'], lw['gla_b2'], lw['gla_on'], lw['gla_ind']]
    return pl.pallas_call(
        functools.partial(_gla_kernel, chunks=cps),
        grid=(bsz, ns),
        in_specs=[col(hk, P_GQ), col(hk, P_GK), col(hv, P_GV), col(hv, P_GR), col(LANES, P_MISC)]
        + [full(a) for a in consts],
        out_specs=pl.BlockSpec((rows, hv), lambda b, c: (b * ns + c, 0)),
        out_shape=jax.ShapeDtypeStruct((bsz * seq, hv), BF16),
        scratch_shapes=[pltpu.VMEM((GLA_HEADS, GLA_VAL_DIM, GLA_KEY_DIM), F32)],
        compiler_params=_cparams(("parallel", "arbitrary"), 32),
        name="gla",
    )(p2, p2, p2, p2, p2, *consts)


def _s5_state_kernel(u_ref, w_ref, l_ref):
    l_ref[...] = _dot(u_ref[...], w_ref[...])


def _s5_chain_kernel(lre_ref, lim_ref, dec_ref, hre_ref, him_ref, *, bsz, nc):
    ar, ai = dec_ref[0:1, :], dec_ref[1:2, :]

    def step(c, carry):
        hr, hi = carry
        at = pl.ds(c, bsz, stride=nc)
        hre_ref[at, :] = hr
        him_ref[at, :] = hi
        return ar * hr - ai * hi + lre_ref[at, :], ar * hi + ai * hr + lim_ref[at, :]

    zero = jnp.zeros((bsz, lre_ref.shape[1]), F32)
    lax.fori_loop(0, nc, step, (zero, zero))


def _s5_read_kernel(u_ref, hre_ref, him_ref, d_ref, vre_ref, vim_ref, y_ref, *, t, width):
    tau = pl.program_id(1)
    acc = _dot(hre_ref[...].astype(BF16), vre_ref[...]) + _dot(him_ref[...].astype(BF16), vim_ref[...])
    for s in range(t):
        acc = acc + _dot(u_ref[:, s * width:(s + 1) * width], d_ref[tau - s + (t - 1)])
    y_ref[...] = acc


def _s5_out_kernel(y_ref, u_ref, d_ref, w_ref, b_ref, o_ref):
    y = y_ref[...] + d_ref[...] * u_ref[...]
    y = 0.5 * y * (1.0 + jnp.tanh(math.sqrt(2.0 / math.pi) * (y + 0.044715 * (y * y * y))))
    gate = jax.nn.sigmoid(_dot(y.astype(BF16), w_ref[...]) + b_ref[...])
    o_ref[...] = (y * gate).astype(BF16)


def _s5_out(y2, p2, lw):
    m, w = y2.shape
    tm = ROW_TILE
    full = lambda a: pl.BlockSpec(a.shape, lambda i: (0,) * a.ndim)
    consts = [lw['s5_d'], lw['s5_w_glu'], lw['s5_b_glu']]
    return pl.pallas_call(
        _s5_out_kernel,
        grid=(m // tm,),
        in_specs=[pl.BlockSpec((tm, w), lambda i: (i, 0)), pl.BlockSpec((tm, w), lambda i: (i, P_SU // w))]
        + [full(a) for a in consts],
        out_specs=pl.BlockSpec((tm, w), lambda i: (i, 0)),
        out_shape=jax.ShapeDtypeStruct((m, w), BF16),
        compiler_params=_cparams(("parallel",), 32),
        name="s5_out",
    )(y2, p2, *consts)


def _s5(p2, lw, bsz, seq):
    t = S5_CHUNK
    bw = lw['s5_d'].shape[-1]
    ns = lw['s5_dec'].shape[-1]
    nc = seq // t
    rows = bsz * nc
    u = p2[:, P_SU:P_SU + bw].astype(BF16).reshape(rows, t * bw)
    tr, tn = min(512, rows), 512
    local = pl.pallas_call(
        _s5_state_kernel,
        grid=(rows // tr, 2 * ns // tn),
        in_specs=[pl.BlockSpec((tr, t * bw), lambda i, n: (i, 0)),
                  pl.BlockSpec((t * bw, tn), lambda i, n: (0, n))],
        out_specs=pl.BlockSpec((tr, tn), lambda i, n: (i, n)),
        out_shape=jax.ShapeDtypeStruct((rows, 2 * ns), F32),
        compiler_params=_cparams(("parallel", "arbitrary"), 40),
        name="s5_state",
    )(u, lw['s5_w'])
    tc = LANES
    nb = ns // tc
    hre, him = pl.pallas_call(
        functools.partial(_s5_chain_kernel, bsz=bsz, nc=nc),
        grid=(nb,),
        in_specs=[pl.BlockSpec((rows, tc), lambda n: (0, n)),
                  pl.BlockSpec((rows, tc), lambda n: (0, n + nb)),
                  pl.BlockSpec((2, tc), lambda n: (0, n))],
        out_specs=[pl.BlockSpec((rows, tc), lambda n: (0, n))] * 2,
        out_shape=[jax.ShapeDtypeStruct((rows, ns), F32)] * 2,
        compiler_params=_cparams(("parallel",), 48),
        name="s5_chain",
    )(local, local, lw['s5_dec'])
    ty = min(256, rows)
    y = pl.pallas_call(
        functools.partial(_s5_read_kernel, t=t, width=bw),
        grid=(rows // ty, t),
        in_specs=[pl.BlockSpec((ty, t * bw), lambda i, tau: (i, 0)),
                  pl.BlockSpec((ty, ns), lambda i, tau: (i, 0)),
                  pl.BlockSpec((ty, ns), lambda i, tau: (i, 0)),
                  pl.BlockSpec(lw['s5_lag'].shape, lambda i, tau: (0, 0, 0)),
                  pl.BlockSpec((ns, bw), lambda i, tau: (0, tau)),
                  pl.BlockSpec((ns, bw), lambda i, tau: (0, tau))],
        out_specs=pl.BlockSpec((ty, bw), lambda i, tau: (i, tau)),
        out_shape=jax.ShapeDtypeStruct((rows, t * bw), F32),
        compiler_params=_cparams(("parallel", "arbitrary"), 48),
        name="s5_read",
    )(u, hre, him, lw['s5_lag'], lw['s5_vre'], lw['s5_vim'])
    return _s5_out(y.reshape(bsz * seq, bw), p2, lw)


def _merge_kernel(h_ref, ba_ref, bb_ref, bc_ref, bd_ref, wg_ref, wb_ref, o_ref):
    h = h_ref[...]
    acc = None
    for i, br in enumerate((ba_ref, bb_ref, bc_ref, bd_ref)):
        t = jax.nn.sigmoid(_dot(h, wg_ref[i])) * _dot(br[...], wb_ref[i])
        acc = t if acc is None else acc + t
    o_ref[...] = acc.astype(BF16)


def _merge(h2, branches, w_gate, w_branch):
    m, d = h2.shape
    bw = w_branch.shape[1]
    tm, tn = ROW_TILE, 512
    row = lambda w: pl.BlockSpec((tm, w), lambda j, i: (i, 0))
    return pl.pallas_call(
        _merge_kernel,
        grid=(d // tn, m // tm),
        in_specs=[row(d), row(bw), row(bw), row(bw), row(bw),
                  pl.BlockSpec((N_BRANCHES, d, tn), lambda j, i: (0, 0, j)),
                  pl.BlockSpec((N_BRANCHES, bw, tn), lambda j, i: (0, 0, j))],
        out_specs=pl.BlockSpec((tm, tn), lambda j, i: (i, j)),
        out_shape=jax.ShapeDtypeStruct((m, d), BF16),
        compiler_params=_cparams(("parallel", "parallel"), 56),
        name="merge",
    )(h2, *branches, w_gate, w_branch)


def _outproj_kernel(x_ref, mod_ref, a_ref, w_ref, o_ref):
    o_ref[...] = x_ref[...] + mod_ref[0, 5:6, :] * _dot(a_ref[...], w_ref[...])


def _outproj(x2, mod_l, merged, w_out, seq):
    m, d = x2.shape
    tm = ROW_TILE
    tiles_per_batch = seq // tm
    return pl.pallas_call(
        _outproj_kernel,
        grid=(m // tm,),
        in_specs=[
            pl.BlockSpec((tm, d), lambda i: (i, 0)),
            pl.BlockSpec((1, N_MOD, d), lambda i: (i // tiles_per_batch, 0, 0)),
            pl.BlockSpec((tm, d), lambda i: (i, 0)),
            pl.BlockSpec((d, d), lambda i: (0, 0)),
        ],
        out_specs=pl.BlockSpec((tm, d), lambda i: (i, 0)),
        out_shape=jax.ShapeDtypeStruct((m, d), F32),
        compiler_params=_cparams(("parallel",), 48),
        name="outproj",
    )(x2, mod_l, merged, w_out)


def _pad_cols(w, n):
    return jnp.pad(w, ((0, 0), (0, n - w.shape[1])))


def _prep_ffn_weights(w_in, w_out):
    d_ff = w_out.shape[0]
    ffp = -(-d_ff // FF_TILE) * FF_TILE
    w_in_p = jnp.concatenate([_pad_cols(w_in[:, :d_ff], ffp), _pad_cols(w_in[:, d_ff:], ffp)], axis=1)
    w_out_p = jnp.pad(w_out, ((0, ffp - d_ff), (0, 0)))
    return w_in_p.astype(BF16), w_out_p.astype(BF16)


def _prep_w_in(w_in):
    sizes = (512, 512, 512, 512, 64, 8, 256, 256, 512, 16, 512, 512, 384, 128, 64)
    offs = np.concatenate([[0], np.cumsum(sizes)])
    seg = lambda i: w_in[:, offs[i]:offs[i + 1]]
    (a_q, a_k, a_v, i_q, i_k, i_w, g_q, g_k, g_v, g_lr, g_r, s_u, m_q, m_kv, m_kr) = [seg(i) for i in range(15)]
    pad = jnp.zeros((w_in.shape[0], P_COLS - P_MISC - 24), w_in.dtype)
    out = jnp.concatenate([a_q, a_k, a_v, i_q, g_q, g_k, g_v, g_r, s_u, m_q, m_kv, i_k, m_kr, i_w, g_lr, pad],
                          axis=1)
    assert out.shape[1] == P_COLS
    return out.astype(BF16)


def _prep_s5(a_re, a_im, log_dt, b_re, b_im, c_re, c_im):
    t = S5_CHUNK
    g, p = a_re.shape
    n_in = b_re.shape[-1]
    dt = jnp.exp(log_dt)[:, None]
    mag = jnp.exp(dt * a_re)
    abar_re, abar_im = mag * jnp.cos(dt * a_im), mag * jnp.sin(dt * a_im)
    den = a_re * a_re + a_im * a_im
    nr, ni = abar_re - 1.0, abar_im
    z_re, z_im = (nr * a_re + ni * a_im) / den, (ni * a_re - nr * a_im) / den
    j = jnp.arange(t + 1, dtype=F32)[None, :, None]
    pmag = jnp.exp(j * (dt * a_re)[:, None, :])
    pw_re = pmag * jnp.cos(j * (dt * a_im)[:, None, :])
    pw_im = pmag * jnp.sin(j * (dt * a_im)[:, None, :])
    zb_re = z_re[:, :, None] * b_re - z_im[:, :, None] * b_im
    zb_im = z_re[:, :, None] * b_im + z_im[:, :, None] * b_re
    e_re = pw_re[..., None] * zb_re[:, None] - pw_im[..., None] * zb_im[:, None]
    e_im = pw_re[..., None] * zb_im[:, None] + pw_im[..., None] * zb_re[:, None]
    hp = lax.Precision.HIGHEST
    eye = jnp.eye(g, dtype=F32)
    kc = (jnp.einsum('gip,gjpn->gjin', c_re, e_re[:, :t], precision=hp)
          - jnp.einsum('gip,gjpn->gjin', c_im, e_im[:, :t], precision=hp))
    lag = jnp.einsum('gjin,gh->jgnhi', kc, eye).reshape(t, g * n_in, g * n_in)
    lag = jnp.concatenate([jnp.zeros((t - 1,) + lag.shape[1:], F32), lag], axis=0)
    def state_w(e):
        w = e[:, :t][:, ::-1].transpose(1, 0, 3, 2)
        return jnp.einsum('sgnp,gh->sgnhp', w, eye).reshape(t * g * n_in, g * p)
    w = jnp.concatenate([state_w(e_re), state_w(e_im)], axis=1)
    q_re, q_im = pw_re[:, 1:], pw_im[:, 1:]
    v_re = c_re[:, None] * q_re[:, :, None, :] - c_im[:, None] * q_im[:, :, None, :]
    v_im = c_re[:, None] * q_im[:, :, None, :] + c_im[:, None] * q_re[:, :, None, :]
    read = lambda v: jnp.einsum('gtip,gh->gpthi', v, eye).reshape(g * p, t * g * n_in)
    dec = jnp.stack([pw_re[:, t].reshape(-1), pw_im[:, t].reshape(-1)])
    return lag.astype(BF16), w.astype(BF16), read(v_re).astype(BF16), read(-v_im).astype(BF16), dec


def _rope_tables(positions):
    pos = positions.astype(F32).reshape(-1, 1)

    def cs(half):
        inv = jnp.power(ROPE_THETA, -jnp.arange(half, dtype=F32) / half)
        ang = pos * inv
        return jnp.cos(ang), jnp.sin(ang)

    c, s = cs(64)
    c128, s128 = jnp.concatenate([c, c], -1), jnp.concatenate([-s, s], -1)
    c, s = cs(32)
    c64, s64 = jnp.concatenate([c, c, c, c], -1), jnp.concatenate([-s, s, -s, s], -1)
    return c128, s128, c64, s64


def _layer_weights(l, prm):
    lw = {}
    lw['ff1'] = _prep_ffn_weights(prm['w_ff1_in'][l], prm['w_ff1_out'][l])
    lw['ff2'] = _prep_ffn_weights(prm['w_ff2_in'][l], prm['w_ff2_out'][l])
    lw['w_in'] = _prep_w_in(prm['w_in'][l])
    lw['dsa_norm'] = prm['dsa_qk_norm'][l]
    lw['mla_q_norm'] = prm['mla_q_norm'][l].reshape(1, -1)
    lw['mla_kv_norm'] = prm['mla_kv_norm'][l].reshape(1, -1)
    nd, rd, vd = MLA_NOPE_DIM, MLA_ROPE_DIM, MLA_V_DIM
    wuq = prm['mla_w_uq'][l].reshape(MLA_Q_RANK, MLA_HEADS, nd + rd)
    lw['w_uq'] = jnp.pad(wuq, ((0, 0), (0, 0), (0, 2 * LANES - nd - rd))).reshape(MLA_Q_RANK, -1).astype(BF16)
    wukv = prm['mla_w_ukv'][l].reshape(MLA_KV_RANK, MLA_HEADS, nd + vd)
    lw['w_ukn'] = wukv[:, :, :nd].reshape(MLA_KV_RANK, -1).astype(BF16)
    lw['w_uv'] = wukv[:, :, nd:].reshape(MLA_KV_RANK, -1).astype(BF16)
    qkn = prm['mla_qk_norm'][l]
    zpad = jnp.zeros((LANES - rd,), F32)
    lw['mla_qk_norm'] = jnp.stack([qkn[0, :nd], jnp.concatenate([qkn[0, nd:], zpad]),
                                   qkn[1, :nd], jnp.concatenate([zpad, qkn[1, nd:]])])
    w2 = jnp.zeros((LANES, GLA_HEADS * GLA_KEY_DIM), F32)
    lw['gla_w2'] = w2.at[MISC_GLR:MISC_GLR + GLA_GATE_RANK].set(prm['gla_w_gate'][l])
    lw['gla_b2'] = prm['gla_b_gate'][l].reshape(1, -1)
    lw['gla_on'] = prm['gla_out_norm'][l].reshape(1, -1)
    ind = np.kron(np.eye(GLA_HEADS, dtype=np.float32), np.ones((GLA_KEY_DIM, GLA_VAL_DIM), np.float32))
    lw['gla_ind'] = jnp.asarray(ind, BF16)
    lw['s5_lag'], lw['s5_w'], lw['s5_vre'], lw['s5_vim'], lw['s5_dec'] = _prep_s5(
        prm['s5_a_re'][l], prm['s5_a_im'][l], prm['s5_log_dt'][l], prm['s5_b_re'][l], prm['s5_b_im'][l],
        prm['s5_c_re'][l], prm['s5_c_im'][l])
    lw['s5_d'] = prm['s5_d'][l].reshape(1, -1)
    lw['s5_w_glu'] = prm['s5_w_glu'][l].astype(BF16)
    lw['s5_b_glu'] = prm['s5_b_glu'][l].reshape(1, -1)
    lw['w_gate'] = prm['w_gate'][l].astype(BF16)
    lw['w_branch'] = prm['w_branch'][l].astype(BF16)
    lw['w_out'] = prm['w_out'][l].astype(BF16)
    return lw


def _mixers(p2, tabs, lw, bsz, seq, k_sel):
    qa, ka, va, iq, ik2, qm, km, vm = _prep(p2, tabs, lw)
    br_a = _dsa(qa, iq, p2, ka, va, ik2, bsz, seq, k_sel)
    br_b = _gla(p2, lw, bsz, seq)
    br_c = _s5(p2, lw, bsz, seq)
    br_d = _mla(qm, km, vm, bsz, seq)
    return br_a, br_b, br_c, br_d


def kernel(x, c, positions, w_ada, b_ada, norm_g, w_ff1_in, w_ff1_out, w_ff2_in, w_ff2_out, w_in, dsa_qk_norm, gla_w_gate, gla_b_gate, gla_out_norm, s5_a_re, s5_a_im, s5_log_dt, s5_b_re, s5_b_im, s5_c_re, s5_c_im, s5_d, s5_w_glu, s5_b_glu, mla_q_norm, mla_kv_norm, mla_w_uq, mla_w_ukv, mla_qk_norm, w_branch, w_gate, w_out):
    prm = dict(w_ff1_in=w_ff1_in, w_ff1_out=w_ff1_out, w_ff2_in=w_ff2_in, w_ff2_out=w_ff2_out, w_in=w_in,
               dsa_qk_norm=dsa_qk_norm, gla_w_gate=gla_w_gate, gla_b_gate=gla_b_gate, gla_out_norm=gla_out_norm,
               s5_a_re=s5_a_re, s5_a_im=s5_a_im, s5_log_dt=s5_log_dt, s5_b_re=s5_b_re, s5_b_im=s5_b_im,
               s5_c_re=s5_c_re, s5_c_im=s5_c_im, s5_d=s5_d, s5_w_glu=s5_w_glu, s5_b_glu=s5_b_glu,
               mla_q_norm=mla_q_norm, mla_kv_norm=mla_kv_norm, mla_w_uq=mla_w_uq, mla_w_ukv=mla_w_ukv,
               mla_qk_norm=mla_qk_norm, w_branch=w_branch, w_gate=w_gate, w_out=w_out)
    bsz, seq, d = x.shape
    depth = w_ada.shape[0]
    k_sel = min(INDEX_TOPK, seq // 4)
    mod = _ada_all(c, w_ada, b_ada).reshape(depth, bsz, N_MOD, d)
    tabs = _rope_tables(positions)
    x2 = x.reshape(bsz * seq, d)
    for l in range(depth):
        lw = _layer_weights(l, prm)
        x2 = _ffn(x2, mod[l], norm_g[l, 0], *lw['ff1'], seq, 0)
        h2, p2 = _inproj(x2, mod[l], norm_g[l, 1], lw['w_in'], seq)
        branches = _mixers(p2, tabs, lw, bsz, seq, k_sel)
        merged = _merge(h2, branches, lw['w_gate'], lw['w_branch'])
        x2 = _outproj(x2, mod[l], merged, lw['w_out'], seq)
        x2 = _ffn(x2, mod[l], norm_g[l, 2], *lw['ff2'], seq, 6)
    return x2.reshape(bsz, seq, d)
```
